```python
import math
import jax
import jax.numpy as jnp
from jax import lax
import numpy as np

D_MODEL = 2048
BATCH = 8
SEQ = 4096
DEPTH = 2

GRID_W = 64
CTX_LEN = 256
HEAD_DIM = 128
GROUP_WIDTH = 512
N_GROUPS = 4
MIX_WIDTH = N_GROUPS * GROUP_WIDTH
GQA_HEADS = 4
GQA_KV_HEADS = 2
DIFF_HEADS = 4
DIFF_QK_DIM = 64
DIFF_V_DIM = 128
WIN_HEADS = 4
WIN_KV_HEADS = 2
WINDOW = 128
MLA_HEADS = 4
MLA_Q_RANK = 512
MLA_KV_RANK = 256
MLA_NOPE = 128
MLA_ROPE = 64
MLA_V = 128
MLA_QK_DIM = MLA_NOPE + MLA_ROPE
D_FF = ((8 * D_MODEL + 3 * 256 - 1) // (3 * 256)) * 256
Q_BLOCK = 128
N_NBR = (WINDOW + Q_BLOCK - 1) // Q_BLOCK
BAND = (2 * N_NBR + 1) * Q_BLOCK
ROPE_BASE = 10000.0
NORM_EPS = 1e-6
NEG_INF = -1e30
IN_SPLITS = (
    GQA_HEADS * HEAD_DIM, GQA_KV_HEADS * HEAD_DIM, GQA_KV_HEADS * HEAD_DIM,
    DIFF_HEADS * 2 * DIFF_QK_DIM, DIFF_HEADS * 2 * DIFF_QK_DIM, DIFF_HEADS * DIFF_V_DIM,
    WIN_HEADS * HEAD_DIM, WIN_KV_HEADS * HEAD_DIM, WIN_KV_HEADS * HEAD_DIM,
    MLA_Q_RANK, MLA_KV_RANK, MLA_ROPE,
)
IN_COLS = sum(IN_SPLITS)

kernel_name = 'hybrid_parallel_heads_flow_block'


def rmsnorm(x, g):
    xf = x.astype(jnp.float32)
    y = xf * lax.rsqrt(jnp.mean(xf * xf, axis=-1, keepdims=True) + NORM_EPS)
    return (y * g.astype(jnp.float32)).astype(x.dtype)


def modulate(h, shift, scale):
    return h * (1.0 + scale) + shift


def softmax_f32(s):
    return jax.nn.softmax(s.astype(jnp.float32), axis=-1)


def axial_rope_tables(n_tokens, dim):
    rows = n_tokens // GRID_W
    row = jnp.broadcast_to(jnp.arange(rows)[:, None], (rows, GRID_W)).reshape(-1).astype(jnp.float32)
    col = jnp.broadcast_to(jnp.arange(GRID_W)[None, :], (rows, GRID_W)).reshape(-1).astype(jnp.float32)
    quarter = dim // 4
    inv_freq = ROPE_BASE ** (-jnp.arange(quarter, dtype=jnp.float32) / quarter)
    ang = jnp.concatenate([row[:, None] * inv_freq, col[:, None] * inv_freq], axis=-1)
    return jnp.cos(ang), jnp.sin(ang)


def apply_rope(x, rope):
    cos, sin = rope
    shape = (cos.shape[0],) + (1,) * (x.ndim - 3) + (cos.shape[1],)
    cos, sin = cos.reshape(shape), sin.reshape(shape)
    xf = x.astype(jnp.float32)
    half = x.shape[-1] // 2
    x1, x2 = xf[..., :half], xf[..., half:]
    return jnp.concatenate([x1 * cos - x2 * sin, x2 * cos + x1 * sin], axis=-1).astype(x.dtype)


def split_columns(y):
    offsets = []
    acc = 0
    for w in IN_SPLITS[:-1]:
        acc += w
        offsets.append(acc)
    return jnp.split(y, offsets, axis=-1)


def to_blocks(a):
    b, t = a.shape[:2]
    return jnp.moveaxis(a.reshape(b, t // Q_BLOCK, Q_BLOCK, *a.shape[2:]), 1, 0)


def from_blocks(a):
    nb, b = a.shape[:2]
    return jnp.moveaxis(a, 0, 1).reshape(b, nb * Q_BLOCK, *a.shape[3:])


def sweep_query_blocks(block_fn, *qs):
    return from_blocks(lax.map(block_fn, tuple(to_blocks(q) for q in qs)))


def dense_gqa(q, k, v, scale):
    def block(args):
        (qb,) = args
        s = jnp.einsum('bqgrd,bkgd->bgrqk', qb, k) * scale
        p = softmax_f32(s).astype(v.dtype)
        return jnp.einsum('bgrqk,bkge->bqgre', p, v)
    o = sweep_query_blocks(block, q)
    return o.reshape(o.shape[0], o.shape[1], -1)


def diff_attention(q1, q2, k1, k2, v, lam, scale):
    def block(args):
        q1b, q2b = args
        p1 = softmax_f32(jnp.einsum('bqhd,bkhd->bhqk', q1b, k1) * scale)
        p2 = softmax_f32(jnp.einsum('bqhd,bkhd->bhqk', q2b, k2) * scale)
        return jnp.einsum('bhqk,bkhe->bqhe', (p1 - lam * p2).astype(v.dtype), v)
    return sweep_query_blocks(block, q1, q2)


def sink_attention_block(qb, k, v, sink, scale, mask):
    s = jnp.einsum('bqgrd,bkgd->bgrqk', qb, k).astype(jnp.float32) * scale
    if mask is not None:
        s = jnp.where(mask, s, NEG_INF)
    sink_col = jnp.broadcast_to(sink.astype(jnp.float32)[None, :, :, None, None], s.shape[:-1] + (1,))
    p = softmax_f32(jnp.concatenate([s, sink_col], axis=-1))[..., :-1]
    return jnp.einsum('bgrqk,bkge->bqgre', p.astype(v.dtype), v)


def windowed_sink_attention(q, k, v, k_ctx, v_ctx, sink, scale):
    b, t = q.shape[:2]
    nb = t // Q_BLOCK
    pad = N_NBR * Q_BLOCK

    def band(a):
        ap = jnp.pad(a, ((0, 0), (pad, pad), (0, 0), (0, 0)))
        ab = ap.reshape(b, nb + 2 * N_NBR, Q_BLOCK, *a.shape[2:])
        banded = jnp.concatenate([ab[:, j:j + nb] for j in range(2 * N_NBR + 1)], axis=2)
        return jnp.moveaxis(banded, 1, 0)

    blk = jnp.arange(nb)[:, None, None]
    q_pos = blk * Q_BLOCK + jnp.arange(Q_BLOCK)[None, :, None]
    k_pos = (blk - N_NBR) * Q_BLOCK + jnp.arange(BAND)[None, None, :]
    band_mask = (jnp.abs(q_pos - k_pos) <= WINDOW) & (k_pos >= 0) & (k_pos < t)
    ctx_mask = jnp.ones((Q_BLOCK, k_ctx.shape[1]), dtype=bool)

    def block(args):
        qb, kb, vb, mb = args
        keys = jnp.concatenate([k_ctx, kb], axis=1)
        vals = jnp.concatenate([v_ctx, vb], axis=1)
        return sink_attention_block(qb, keys, vals, sink, scale, jnp.concatenate([ctx_mask, mb], axis=-1))

    o = from_blocks(lax.map(block, (to_blocks(q), band(k), band(v), band_mask)))
    return o.reshape(b, t, -1)


def mixer_gqa(q, k, v, qc, kc, vc, q_gain, k_gain, rope, need_ctx_out):
    rep = GQA_HEADS // GQA_KV_HEADS
    scale = HEAD_DIM ** -0.5

    def prep(q, k, v, rotate):
        b, t = q.shape[:2]
        q = rmsnorm(q.reshape(b, t, GQA_HEADS, HEAD_DIM), q_gain)
        k = rmsnorm(k.reshape(b, t, GQA_KV_HEADS, HEAD_DIM), k_gain)
        if rotate:
            q, k = apply_rope(q, rope), apply_rope(k, rope)
        return q.reshape(b, t, GQA_KV_HEADS, rep, HEAD_DIM), k, v.reshape(b, t, GQA_KV_HEADS, HEAD_DIM)

    q, k, v = prep(q, k, v, True)
    qc, kc, vc = prep(qc, kc, vc, False)
    out = dense_gqa(q, jnp.concatenate([kc, k], axis=1), jnp.concatenate([vc, v], axis=1), scale)
    out_c = dense_gqa(qc, kc, vc, scale) if need_ctx_out else None
    return out, out_c


def mixer_diff(q, k, v, qc, kc, vc, lq1, lk1, lq2, lk2, subln, lam_init, rope, need_ctx_out):
    scale = DIFF_QK_DIM ** -0.5
    f32 = jnp.float32
    lam = (jnp.exp(jnp.sum(lq1.astype(f32) * lk1.astype(f32)))
           - jnp.exp(jnp.sum(lq2.astype(f32) * lk2.astype(f32))) + lam_init)

    def prep(q, k, v, rotate):
        b, t = q.shape[:2]
        q = q.reshape(b, t, DIFF_HEADS, 2, DIFF_QK_DIM)
        k = k.reshape(b, t, DIFF_HEADS, 2, DIFF_QK_DIM)
        q1, q2, k1, k2 = q[:, :, :, 0], q[:, :, :, 1], k[:, :, :, 0], k[:, :, :, 1]
        if rotate:
            q1, q2, k1, k2 = (apply_rope(a, rope) for a in (q1, q2, k1, k2))
        return q1, q2, k1, k2, v.reshape(b, t, DIFF_HEADS, DIFF_V_DIM)

    def finish(o):
        o = rmsnorm(o, subln) * (1.0 - lam_init)
        return o.reshape(o.shape[0], o.shape[1], -1)

    q1, q2, k1, k2, v = prep(q, k, v, True)
    q1c, q2c, k1c, k2c, vc = prep(qc, kc, vc, False)
    out = diff_attention(q1, q2, jnp.concatenate([k1c, k1], axis=1), jnp.concatenate([k2c, k2], axis=1),
                         jnp.concatenate([vc, v], axis=1), lam, scale)
    out_c = finish(diff_attention(q1c, q2c, k1c, k2c, vc, lam, scale)) if need_ctx_out else None
    return finish(out), out_c


def mixer_window(q, k, v, qc, kc, vc, sinks, rope, need_ctx_out):
    rep = WIN_HEADS // WIN_KV_HEADS
    scale = HEAD_DIM ** -0.5
    sink = sinks.reshape(WIN_KV_HEADS, rep)

    def prep(q, k, v, rotate):
        b, t = q.shape[:2]
        q = q.reshape(b, t, WIN_HEADS, HEAD_DIM)
        k = k.reshape(b, t, WIN_KV_HEADS, HEAD_DIM)
        if rotate:
            q, k = apply_rope(q, rope), apply_rope(k, rope)
        return q.reshape(b, t, WIN_KV_HEADS, rep, HEAD_DIM), k, v.reshape(b, t, WIN_KV_HEADS, HEAD_DIM)

    q, k, v = prep(q, k, v, True)
    qc, kc, vc = prep(qc, kc, vc, False)
    out = windowed_sink_attention(q, k, v, kc, vc, sink, scale)
    out_c = None
    if need_ctx_out:
        oc = sink_attention_block(qc, kc, vc, sink, scale, None)
        out_c = oc.reshape(oc.shape[0], oc.shape[1], -1)
    return out, out_c


def mixer_mla(cq, ckv, kr, cqc, ckvc, krc, q_norm, w_uq, kv_norm, w_ukv, rope, need_ctx_out):
    scale = MLA_QK_DIM ** -0.5

    def prep(cq, ckv, kr, rotate):
        b, t = cq.shape[:2]
        q = (rmsnorm(cq, q_norm) @ w_uq).reshape(b, t, MLA_HEADS, MLA_QK_DIM)
        kv = (rmsnorm(ckv, kv_norm) @ w_ukv).reshape(b, t, MLA_HEADS, MLA_NOPE + MLA_V)
        q_nope, q_rot = q[..., :MLA_NOPE], q[..., MLA_NOPE:]
        k_nope, v = kv[..., :MLA_NOPE], kv[..., MLA_NOPE:]
        if rotate:
            q_rot, kr = apply_rope(q_rot, rope), apply_rope(kr, rope)
        k_rot = jnp.broadcast_to(kr[:, :, None, :], (b, t, MLA_HEADS, MLA_ROPE))
        q = jnp.concatenate([q_nope, q_rot], axis=-1).reshape(b, t, MLA_HEADS, 1, MLA_QK_DIM)
        return q, jnp.concatenate([k_nope, k_rot], axis=-1), v

    q, k, v = prep(cq, ckv, kr, True)
    qc, kc, vc = prep(cqc, ckvc, krc, False)
    out = dense_gqa(q, jnp.concatenate([kc, k], axis=1), jnp.concatenate([vc, v], axis=1), scale)
    out_c = dense_gqa(qc, kc, vc, scale) if need_ctx_out else None
    return out, out_c


def swiglu(h, w_gate, w_up, w_down):
    return (jax.nn.silu(h @ w_gate) * (h @ w_up)) @ w_down


def trunk_layer(x, ctx, mod_lat, mod_ctx, lp, ropes, lam_init, need_ctx_out):
    rope_hd, rope_diff, rope_mla = ropes
    m = jnp.split(mod_lat, 6, axis=-1)
    mc = jnp.split(mod_ctx, 6, axis=-1)
    h = modulate(rmsnorm(x, lp['norm_pre_mix']), m[0], m[1])
    hc = modulate(rmsnorm(ctx, lp['norm_pre_mix']), mc[0], mc[1])
    p = split_columns(h @ lp['w_in'])
    pc = split_columns(hc @ lp['w_in'])
    a, ac = mixer_gqa(p[0], p[1], p[2], pc[0], pc[1], pc[2], lp['gqa_q_norm'], lp['gqa_k_norm'],
                      rope_hd, need_ctx_out)
    b, bc = mixer_diff(p[3], p[4], p[5], pc[3], pc[4], pc[5], lp['diff_lambda_q1'], lp['diff_lambda_k1'],
                       lp['diff_lambda_q2'], lp['diff_lambda_k2'], lp['diff_subln'], lam_init,
                       rope_diff, need_ctx_out)
    w, wc = mixer_window(p[6], p[7], p[8], pc[6], pc[7], pc[8], lp['win_sinks'], rope_hd, need_ctx_out)
    d, dc = mixer_mla(p[9], p[10], p[11], pc[9], pc[10], pc[11], lp['mla_q_norm'], lp['mla_w_uq'],
                      lp['mla_kv_norm'], lp['mla_w_ukv'], rope_mla, need_ctx_out)
    y = jnp.concatenate([a, b, w, d], axis=-1) @ lp['w_out']
    x = x + m[2] * rmsnorm(y, lp['norm_post_mix'])
    h = modulate(rmsnorm(x, lp['norm_pre_ffn']), m[3], m[4])
    x = x + m[5] * rmsnorm(swiglu(h, lp['ffn_w_gate'], lp['ffn_w_up'], lp['ffn_w_down']), lp['norm_post_ffn'])
    if need_ctx_out:
        yc = jnp.concatenate([ac, bc, wc, dc], axis=-1) @ lp['w_out']
        ctx = ctx + mc[2] * rmsnorm(yc, lp['norm_post_mix'])
        hc = modulate(rmsnorm(ctx, lp['norm_pre_ffn']), mc[3], mc[4])
        ctx = ctx + mc[5] * rmsnorm(swiglu(hc, lp['ffn_w_gate'], lp['ffn_w_up'], lp['ffn_w_down']),
                                    lp['norm_post_ffn'])
    return x, ctx


def setup_inputs(seed: int = 0) -> dict:
    key = jax.random.key(seed)
    ks = jax.random.split(key, 27)
    f32 = jnp.float32
    L = DEPTH

    def normal(k, shape, scale):
        return jax.random.normal(k, shape, f32) * scale

    def gain(k, shape):
        return 1.0 + 0.05 * jax.random.normal(k, shape, f32)

    return {
        'x': normal(ks[0], (BATCH, SEQ, D_MODEL), 1.0),
        'c': normal(ks[1], (BATCH, D_MODEL), 1.0),
        'ctx': normal(ks[2], (BATCH, CTX_LEN, D_MODEL), 1.0),
        'c_ctx': normal(ks[3], (D_MODEL,), 1.0),
        'ada_w': normal(ks[4], (L, D_MODEL, 6 * D_MODEL), 0.5 * D_MODEL ** -0.5),
        'ada_b': normal(ks[5], (L, 6 * D_MODEL), 0.01),
        'norm_pre_mix': gain(ks[6], (L, D_MODEL)),
        'norm_post_mix': gain(ks[7], (L, D_MODEL)),
        'norm_pre_ffn': gain(ks[8], (L, D_MODEL)),
        'norm_post_ffn': gain(ks[9], (L, D_MODEL)),
        'w_in': normal(ks[10], (L, D_MODEL, IN_COLS), D_MODEL ** -0.5),
        'gqa_q_norm': gain(ks[11], (L, HEAD_DIM)),
        'gqa_k_norm': gain(ks[12], (L, HEAD_DIM)),
        'diff_lambda_q1': normal(ks[13], (L, DIFF_QK_DIM), 0.1),
        'diff_lambda_k1': normal(ks[14], (L, DIFF_QK_DIM), 0.1),
        'diff_lambda_q2': normal(ks[15], (L, DIFF_QK_DIM), 0.1),
        'diff_lambda_k2': normal(ks[16], (L, DIFF_QK_DIM), 0.1),
        'diff_subln': gain(ks[17], (L, DIFF_V_DIM)),
        'win_sinks': normal(ks[18], (L, WIN_HEADS), 0.5),
        'mla_q_norm': gain(ks[19], (L, MLA_Q_RANK)),
        'mla_w_uq': normal(ks[20], (L, MLA_Q_RANK, MLA_HEADS * MLA_QK_DIM), MLA_Q_RANK ** -0.5),
        'mla_kv_norm': gain(ks[21], (L, MLA_KV_RANK)),
        'mla_w_ukv': normal(ks[22], (L, MLA_KV_RANK, MLA_HEADS * (MLA_NOPE + MLA_V)), MLA_KV_RANK ** -0.5),
        'w_out': normal(ks[23], (L, MIX_WIDTH, D_MODEL), MIX_WIDTH ** -0.5),
        'ffn_w_gate': normal(ks[24], (L, D_MODEL, D_FF), D_MODEL ** -0.5),
        'ffn_w_up': normal(ks[25], (L, D_MODEL, D_FF), D_MODEL ** -0.5),
        'ffn_w_down': normal(ks[26], (L, D_FF, D_MODEL), D_FF ** -0.5),
    }


def reference(x, c, ctx, c_ctx, ada_w, ada_b, norm_pre_mix, norm_post_mix, norm_pre_ffn, norm_post_ffn,
              w_in, gqa_q_norm, gqa_k_norm, diff_lambda_q1, diff_lambda_k1, diff_lambda_q2, diff_lambda_k2,
              diff_subln, win_sinks, mla_q_norm, mla_w_uq, mla_kv_norm, mla_w_ukv, w_out,
              ffn_w_gate, ffn_w_up, ffn_w_down):
    n_tokens = x.shape[1]
    ropes = (axial_rope_tables(n_tokens, HEAD_DIM), axial_rope_tables(n_tokens, DIFF_QK_DIM),
             axial_rope_tables(n_tokens, MLA_ROPE))
    silu_c = jax.nn.silu(c)
    silu_cc = jax.nn.silu(c_ctx)
    for l in range(DEPTH):
        mod_lat = (silu_c @ ada_w[l] + ada_b[l])[:, None, :]
        mod_ctx = (silu_cc @ ada_w[l] + ada_b[l])[None, None, :]
        lp = {
            'norm_pre_mix': norm_pre_mix[l], 'norm_post_mix': norm_post_mix[l],
            'norm_pre_ffn': norm_pre_ffn[l], 'norm_post_ffn': norm_post_ffn[l],
            'w_in': w_in[l], 'gqa_q_norm': gqa_q_norm[l], 'gqa_k_norm': gqa_k_norm[l],
            'diff_lambda_q1': diff_lambda_q1[l], 'diff_lambda_k1': diff_lambda_k1[l],
            'diff_lambda_q2': diff_lambda_q2[l], 'diff_lambda_k2': diff_lambda_k2[l],
            'diff_subln': diff_subln[l], 'win_sinks': win_sinks[l],
            'mla_q_norm': mla_q_norm[l], 'mla_w_uq': mla_w_uq[l],
            'mla_kv_norm': mla_kv_norm[l], 'mla_w_ukv': mla_w_ukv[l],
            'w_out': w_out[l], 'ffn_w_gate': ffn_w_gate[l], 'ffn_w_up': ffn_w_up[l],
            'ffn_w_down': ffn_w_down[l],
        }
        lam_init = 0.8 - 0.6 * math.exp(-0.3 * l)
        x, ctx = trunk_layer(x, ctx, mod_lat, mod_ctx, lp, ropes, lam_init, l < DEPTH - 1)
    return x
```

```python
import functools
import math

import jax
import jax.numpy as jnp
from jax import lax
from jax.experimental import pallas as pl
from jax.experimental.pallas import tpu as pltpu

F32 = jnp.float32
BF16 = jnp.bfloat16

GRID_W = 64
HEAD_DIM = 128
GQA_HEADS, GQA_KV_HEADS = 4, 2
DIFF_HEADS, DIFF_QK_DIM = 4, 64
WIN_HEADS, WIN_KV_HEADS, WINDOW = 4, 2, 128
MLA_HEADS, MLA_Q_RANK, MLA_KV_RANK = 4, 512, 256
MLA_NOPE, MLA_ROPE, MLA_V = 128, 64, 128
MLA_QK_DIM = MLA_NOPE + MLA_ROPE
MLA_QK_PAD = 256
GROUP_WIDTH = 512
ROPE_BASE = 10000.0
NORM_EPS = 1e-6
NEG_INF = -1e30
LOG2E = math.log2(math.e)
LANES = 128

IN_COLS = 4416
P_COLS = 4608
C_GQ_Q, C_GQ_K, C_GQ_V = 0, 512, 768
C_DF_Q, C_DF_K, C_DF_V = 1024, 1536, 2048
C_WN_Q, C_WN_K, C_WN_V = 2560, 3072, 3328
C_ML_CQ, C_ML_CKV, C_ML_KR = 3584, 4096, 4352

A_GQ_Q, A_GQ_K, A_GQ_V = 0, 4, 6
A_DF_Q, A_DF_K, A_DF_V = 8, 16, 20
A_WN_Q, A_WN_K, A_WN_V = 24, 28, 30
A_ML_V = 32
A_SLOTS = 36
M_Q, M_K = 0, 4
M_SLOTS = 8

QS_HD = HEAD_DIM ** -0.5 * LOG2E
QS_DIFF = DIFF_QK_DIM ** -0.5 * LOG2E
QS_MLA = MLA_QK_DIM ** -0.5 * LOG2E

VMEM_LIMIT = 56 * 1024 * 1024


def _cparams(sem):
    return pltpu.CompilerParams(dimension_semantics=sem, vmem_limit_bytes=VMEM_LIMIT)


def _rms(x, g):
    return x * lax.rsqrt(jnp.mean(x * x, axis=-1, keepdims=True) + NORM_EPS) * g


def _dot(a, b):
    return jnp.dot(a, b, preferred_element_type=F32)


def _dot_nt(a, b):
    return lax.dot_general(a, b, (((1,), (1,)), ((), ())), preferred_element_type=F32)


def _mod_kernel(c_ref, w_ref, b_ref, o_ref):
    c = c_ref[...]
    s = (c / (1.0 + jnp.exp(-c))).astype(BF16)
    o_ref[...] = _dot(s, w_ref[...].astype(BF16)) + b_ref[...]


def _modulation(cc, ada_w, ada_b):
    n_layers, d, d6 = ada_w.shape
    rows = cc.shape[0]
    tn = d // 2
    return pl.pallas_call(
        _mod_kernel,
        grid=(n_layers, d6 // tn),
        in_specs=[
            pl.BlockSpec((rows, d), lambda l, j: (0, 0)),
            pl.BlockSpec((None, d, tn), lambda l, j: (l, 0, j)),
            pl.BlockSpec((None, 1, tn), lambda l, j: (l, 0, j)),
        ],
        out_specs=pl.BlockSpec((None, rows, tn), lambda l, j: (l, 0, j)),
        out_shape=jax.ShapeDtypeStruct((n_layers, rows, d6), F32),
        compiler_params=_cparams(("parallel", "parallel")),
    )(cc, ada_w, ada_b.reshape(n_layers, 1, d6))


def _inproj_kernel(x_ref, mod_ref, g_ref, w_ref, o_ref, h_scr):
    @pl.when(pl.program_id(1) == 0)
    def _():
        h = _rms(x_ref[...], g_ref[...])
        h_scr[...] = (h * (1.0 + mod_ref[1:2, :]) + mod_ref[0:1, :]).astype(BF16)

    o_ref[...] = _dot(h_scr[...], w_ref[...]).astype(BF16)


def _inproj(x, mod, gain, w, tm, mod_idx):
    n, d = x.shape
    tn = 512
    return pl.pallas_call(
        _inproj_kernel,
        grid=(n // tm, P_COLS // tn),
        in_specs=[
            pl.BlockSpec((tm, d), lambda i, j: (i, 0)),
            pl.BlockSpec((None, 6, d), lambda i, j: (mod_idx(i), 0, 0)),
            pl.BlockSpec((1, d), lambda i, j: (0, 0)),
            pl.BlockSpec((d, tn), lambda i, j: (0, j)),
        ],
        out_specs=pl.BlockSpec((tm, tn), lambda i, j: (i, j)),
        out_shape=jax.ShapeDtypeStruct((n, P_COLS), BF16),
        scratch_shapes=[pltpu.VMEM((tm, d), BF16)],
        compiler_params=_cparams(("parallel", "arbitrary")),
    )(x, mod, gain, w)


def _prep_kernel(*refs, rotate):
    if rotate:
        p_ref, c128_ref, s128_ref, c64_ref, s64_ref = refs[:5]
        refs = refs[5:]
    else:
        p_ref = refs[0]
        refs = refs[1:]
    gq_ref, gk_ref, qn_ref, wuq_ref, kvn_ref, wukv_ref, a_ref, m_ref = refs
    tp = p_ref.shape[0]

    def cols(lo, width=LANES):
        return p_ref[:, lo:lo + width].astype(F32)

    if rotate:
        lane = lax.broadcasted_iota(jnp.int32, (tp, LANES), 1)
        low32 = (lane & 63) < 32

        def rope128(x):
            return x * c128_ref[...] + pltpu.roll(x, 64, 1) * s128_ref[...]

        def rope64(x):
            rot = jnp.where(low32, pltpu.roll(x, 96, 1), pltpu.roll(x, 32, 1))
            return x * c64_ref[...] + rot * s64_ref[...]
    else:
        rope128 = rope64 = lambda x: x

    first_half = lax.broadcasted_iota(jnp.int32, (tp, LANES), 1) < 64

    for h in range(GQA_HEADS):
        x = rope128(_rms(cols(C_GQ_Q + LANES * h), gq_ref[...]))
        a_ref[A_GQ_Q + h] = (x * QS_HD).astype(BF16)
    for g in range(GQA_KV_HEADS):
        x = rope128(_rms(cols(C_GQ_K + LANES * g), gk_ref[...]))
        a_ref[A_GQ_K + g] = x.astype(BF16)
        a_ref[A_GQ_V + g] = p_ref[:, C_GQ_V + LANES * g:C_GQ_V + LANES * (g + 1)]

    for h in range(DIFF_HEADS):
        x = rope64(cols(C_DF_Q + LANES * h)) * QS_DIFF
        a_ref[A_DF_Q + 2 * h] = jnp.where(first_half, x, 0.0).astype(BF16)
        a_ref[A_DF_Q + 2 * h + 1] = jnp.where(first_half, 0.0, x).astype(BF16)
        a_ref[A_DF_K + h] = rope64(cols(C_DF_K + LANES * h)).astype(BF16)
        a_ref[A_DF_V + h] = p_ref[:, C_DF_V + LANES * h:C_DF_V + LANES * (h + 1)]

    for h in range(WIN_HEADS):
        a_ref[A_WN_Q + h] = (rope128(cols(C_WN_Q + LANES * h)) * QS_HD).astype(BF16)
    for g in range(WIN_KV_HEADS):
        a_ref[A_WN_K + g] = rope128(cols(C_WN_K + LANES * g)).astype(BF16)
        a_ref[A_WN_V + g] = p_ref[:, C_WN_V + LANES * g:C_WN_V + LANES * (g + 1)]

    hq = _rms(cols(C_ML_CQ, MLA_Q_RANK), qn_ref[...]).astype(BF16)
    qa = _dot(hq, wuq_ref[...])
    hkv = _rms(cols(C_ML_CKV, MLA_KV_RANK), kvn_ref[...]).astype(BF16)
    kva = _dot(hkv, wukv_ref[...])
    kr = rope64(cols(C_ML_KR)).astype(BF16)
    for h in range(MLA_HEADS):
        lo = MLA_QK_PAD * h
        m_ref[M_Q + h, :, 0:LANES] = (qa[:, lo:lo + LANES] * QS_MLA).astype(BF16)
        m_ref[M_Q + h, :, LANES:2 * LANES] = (
            rope64(qa[:, lo + LANES:lo + 2 * LANES]) * QS_MLA).astype(BF16)
        m_ref[M_K + h, :, 0:LANES] = kva[:, LANES * h:LANES * (h + 1)].astype(BF16)
        m_ref[M_K + h, :, LANES:2 * LANES] = kr
        v_lo = MLA_HEADS * MLA_NOPE + MLA_V * h
        a_ref[A_ML_V + h] = kva[:, v_lo:v_lo + MLA_V].astype(BF16)


def _prep(p, tables, lp, tp, tiles_per_seq):
    n = p.shape[0]
    rotate = tables is not None
    const = lambda i: (0, 0)
    in_specs = [pl.BlockSpec((tp, P_COLS), lambda i: (i, 0))]
    args = [p]
    if rotate:
        tab = pl.BlockSpec((tp, LANES), lambda i: (i % tiles_per_seq, 0))
        in_specs += [tab] * 4
        args += list(tables)
    in_specs += [
        pl.BlockSpec((1, HEAD_DIM), const),
        pl.BlockSpec((1, HEAD_DIM), const),
        pl.BlockSpec((1, MLA_Q_RANK), const),
        pl.BlockSpec((MLA_Q_RANK, MLA_HEADS * MLA_QK_PAD), const),
        pl.BlockSpec((1, MLA_KV_RANK), const),
        pl.BlockSpec((MLA_KV_RANK, MLA_HEADS * (MLA_NOPE + MLA_V)), const),
    ]
    args += [lp['gqa_q_norm'], lp['gqa_k_norm'], lp['mla_q_norm'], lp['w_uq'],
             lp['mla_kv_norm'], lp['w_ukv']]
    return pl.pallas_call(
        functools.partial(_prep_kernel, rotate=rotate),
        grid=(n // tp,),
        in_specs=in_specs,
        out_specs=[
            pl.BlockSpec((A_SLOTS, tp, LANES), lambda i: (0, i, 0)),
            pl.BlockSpec((M_SLOTS, tp, MLA_QK_PAD), lambda i: (0, i, 0)),
        ],
        out_shape=[
            jax.ShapeDtypeStruct((A_SLOTS, n, LANES), BF16),
            jax.ShapeDtypeStruct((M_SLOTS, n, MLA_QK_PAD), BF16),
        ],
        compiler_params=_cparams(("parallel",)),
    )(*args)


def _flash_kernel(*refs, mode, has_latent, tk, lam_init):
    q_ref, kc_ref, vc_ref = refs[:3]
    refs = refs[3:]
    if has_latent:
        kl_ref, vl_ref = refs[:2]
        refs = refs[2:]
    if mode == 'diff':
        lq1_ref, lk1_ref, lq2_ref, lk2_ref, sub_ref = refs[:5]
        refs = refs[5:]
    o_ref, m_scr, l_scr, acc_scr = refs
    n_rep, tq, dk = q_ref.shape
    dv = vc_ref.shape[-1]
    q = q_ref[...].reshape(n_rep * tq, dk)

    s = _dot_nt(q, kc_ref[...])
    m0 = jnp.max(s, axis=-1, keepdims=True)
    p = jnp.exp2(s - m0)
    m_scr[...] = m0
    l_scr[...] = jnp.sum(p, axis=-1, keepdims=True)
    acc_scr[...] = _dot(p.astype(BF16), vc_ref[...])

    if has_latent:
        def body(c, carry):
            start = pl.multiple_of(c * tk, tk)
            s = _dot_nt(q, kl_ref[pl.ds(start, tk), :])
            m_old = m_scr[...]
            m_new = jnp.maximum(m_old, jnp.max(s, axis=-1, keepdims=True))
            alpha = jnp.exp2(m_old - m_new)
            p = jnp.exp2(s - m_new)
            m_scr[...] = m_new
            l_scr[...] = alpha * l_scr[...] + jnp.sum(p, axis=-1, keepdims=True)
            acc_scr[...] = alpha * acc_scr[...] + _dot(p.astype(BF16), vl_ref[pl.ds(start, tk), :])
            return carry

        lax.fori_loop(0, kl_ref.shape[0] // tk, body, 0)

    o = acc_scr[...] / l_scr[...]
    if mode == 'plain':
        for r in range(n_rep):
            o_ref[:, dv * r:dv * (r + 1)] = o[tq * r:tq * (r + 1)].astype(o_ref.dtype)
    else:
        lam = (jnp.exp(jnp.sum(lq1_ref[...] * lk1_ref[...], keepdims=True))
               - jnp.exp(jnp.sum(lq2_ref[...] * lk2_ref[...], keepdims=True)) + lam_init)
        d = o[:tq] - lam * o[tq:]
        o_ref[...] = (_rms(d, sub_ref[...]) * (1.0 - lam_init)).astype(o_ref.dtype)


def _flash(q_arr, q_slot, n_rep, k_lat, k_ctx, k_slot, v_lat, v_ctx, v_slot, n_kv, out_w,
           batch, q_is_ctx, tq, mode='plain', diff_params=None, lam_init=0.0):
    dk = q_arr.shape[-1]
    dv = v_ctx.shape[-1]
    n_q = q_arr.shape[1]
    t_ctx = k_ctx.shape[1] // batch
    nq = n_q // batch // tq
    has_latent = not q_is_ctx
    q_blk = q_slot // n_rep
    in_specs = [
        pl.BlockSpec((n_rep, tq, dk), lambda b, g, i: (q_blk + g, b * nq + i, 0)),
        pl.BlockSpec((None, t_ctx, dk), lambda b, g, i: (k_slot + g, b, 0)),
        pl.BlockSpec((None, t_ctx, dv), lambda b, g, i: (v_slot + g, b, 0)),
    ]
    args = [q_arr, k_ctx, v_ctx]
    tk = 0
    if has_latent:
        t_lat = k_lat.shape[1] // batch
        tk = min(512, t_lat)
        in_specs += [
            pl.BlockSpec((None, t_lat, dk), lambda b, g, i: (k_slot + g, b, 0)),
            pl.BlockSpec((None, t_lat, dv), lambda b, g, i: (v_slot + g, b, 0)),
        ]
        args += [k_lat, v_lat]
    if mode == 'diff':
        in_specs += [pl.BlockSpec((1, DIFF_QK_DIM), lambda b, g, i: (0, 0))] * 4
        in_specs += [pl.BlockSpec((1, dv), lambda b, g, i: (0, 0))]
        args += list(diff_params)
    rows = n_rep * tq
    return pl.pallas_call(
        functools.partial(_flash_kernel, mode=mode, has_latent=has_latent, tk=tk, lam_init=lam_init),
        grid=(batch, n_kv, nq),
        in_specs=in_specs,
        out_specs=pl.BlockSpec((tq, out_w), lambda b, g, i: (b * nq + i, g)),
        out_shape=jax.ShapeDtypeStruct((n_q, n_kv * out_w), BF16),
        scratch_shapes=[pltpu.VMEM((rows, 1), F32), pltpu.VMEM((rows, 1), F32),
                        pltpu.VMEM((rows, dv), F32)],
        compiler_params=_cparams(("parallel", "parallel", "arbitrary")),
    )(*args)


def _win_kernel(*refs, has_band, band_w, t_lat):
    sink_ref, q_ref, kc_ref, vc_ref = refs[:4]
    refs = refs[4:]
    if has_band:
        kl_ref, vl_ref = refs[:2]
        refs = refs[2:]
    (o_ref,) = refs
    n_rep, tq, dk = q_ref.shape
    dv = vc_ref.shape[-1]
    rows = n_rep * tq
    g = pl.program_id(1)
    i = pl.program_id(2)
    q = q_ref[...].reshape(rows, dk)

    row = lax.broadcasted_iota(jnp.int32, (rows, 1), 0)
    sink = jnp.zeros((rows, 1), F32)
    for r in range(n_rep):
        in_r = (row >= tq * r) & (row < tq * (r + 1))
        sink = jnp.where(in_r, sink_ref[g * n_rep + r] * LOG2E, sink)

    s_c = _dot_nt(q, kc_ref[...])
    m = jnp.maximum(jnp.max(s_c, axis=-1, keepdims=True), sink)
    if has_band:
        start = pl.multiple_of(jnp.clip(i * tq - WINDOW, 0, t_lat - band_w), LANES)
        s_b = _dot_nt(q, kl_ref[pl.ds(start, band_w), :])
        q_pos = i * tq + (lax.broadcasted_iota(jnp.int32, (rows, band_w), 0) & (tq - 1))
        k_pos = start + lax.broadcasted_iota(jnp.int32, (rows, band_w), 1)
        s_b = jnp.where(jnp.abs(q_pos - k_pos) <= WINDOW, s_b, NEG_INF)
        m = jnp.maximum(m, jnp.max(s_b, axis=-1, keepdims=True))
    p_c = jnp.exp2(s_c - m)
    l = jnp.sum(p_c, axis=-1, keepdims=True) + jnp.exp2(sink - m)
    acc = _dot(p_c.astype(BF16), vc_ref[...])
    if has_band:
        p_b = jnp.exp2(s_b - m)
        l = l + jnp.sum(p_b, axis=-1, keepdims=True)
        acc = acc + _dot(p_b.astype(BF16), vl_ref[pl.ds(start, band_w), :])
    o = acc / l
    for r in range(n_rep):
        o_ref[:, dv * r:dv * (r + 1)] = o[tq * r:tq * (r + 1)].astype(o_ref.dtype)


def _window(sinks, a_q, a_lat, a_ctx, batch, q_is_ctx, tq):
    n_rep = WIN_HEADS // WIN_KV_HEADS
    n_q = a_q.shape[1]
    t_ctx = a_ctx.shape[1] // batch
    nq = n_q // batch // tq
    has_band = not q_is_ctx
    q_blk = A_WN_Q // n_rep
    in_specs = [
        pl.BlockSpec(memory_space=pltpu.SMEM),
        pl.BlockSpec((n_rep, tq, HEAD_DIM), lambda b, g, i: (q_blk + g, b * nq + i, 0)),
        pl.BlockSpec((None, t_ctx, HEAD_DIM), lambda b, g, i: (A_WN_K + g, b, 0)),
        pl.BlockSpec((None, t_ctx, HEAD_DIM), lambda b, g, i: (A_WN_V + g, b, 0)),
    ]
    args = [sinks, a_q, a_ctx, a_ctx]
    band_w = t_lat = 0
    if has_band:
        t_lat = a_lat.shape[1] // batch
        band_w = min(tq + 2 * WINDOW, t_lat)
        assert tq & (tq - 1) == 0
        in_specs += [
            pl.BlockSpec((None, t_lat, HEAD_DIM), lambda b, g, i: (A_WN_K + g, b, 0)),
            pl.BlockSpec((None, t_lat, HEAD_DIM), lambda b, g, i: (A_WN_V + g, b, 0)),
        ]
        args += [a_lat, a_lat]
    return pl.pallas_call(
        functools.partial(_win_kernel, has_band=has_band, band_w=band_w, t_lat=t_lat),
        grid=(batch, WIN_KV_HEADS, nq),
        in_specs=in_specs,
        out_specs=pl.BlockSpec((tq, n_rep * HEAD_DIM), lambda b, g, i: (b * nq + i, g)),
        out_shape=jax.ShapeDtypeStruct((n_q, GROUP_WIDTH), BF16),
        compiler_params=_cparams(("parallel", "parallel", "arbitrary")),
    )(*args)


def _outproj_kernel(a0_ref, a1_ref, a2_ref, a3_ref, w_ref, x_ref, mod_ref, gpost_ref, gpre_ref,
                    xo_ref, h_ref):
    y = None
    for k, a_ref in enumerate((a0_ref, a1_ref, a2_ref, a3_ref)):
        part = _dot(a_ref[...], w_ref[GROUP_WIDTH * k:GROUP_WIDTH * (k + 1), :])
        y = part if y is None else y + part
    xm = x_ref[...] + mod_ref[2:3, :] * _rms(y, gpost_ref[...])
    xo_ref[...] = xm
    h_ref[...] = (_rms(xm, gpre_ref[...]) * (1.0 + mod_ref[4:5, :]) + mod_ref[3:4, :]).astype(BF16)


def _outproj(mix, w_out, x, mod, g_post, g_pre, tm, mod_idx):
    n, d = x.shape
    const = lambda i: (0, 0)
    return pl.pallas_call(
        _outproj_kernel,
        grid=(n // tm,),
        in_specs=[pl.BlockSpec((tm, GROUP_WIDTH), lambda i: (i, 0))] * 4 + [
            pl.BlockSpec(w_out.shape, const),
            pl.BlockSpec((tm, d), lambda i: (i, 0)),
            pl.BlockSpec((None, 6, d), lambda i: (mod_idx(i), 0, 0)),
            pl.BlockSpec((1, d), const),
            pl.BlockSpec((1, d), const),
        ],
        out_specs=[pl.BlockSpec((tm, d), lambda i: (i, 0)), pl.BlockSpec((tm, d), lambda i: (i, 0))],
        out_shape=[jax.ShapeDtypeStruct((n, d), F32), jax.ShapeDtypeStruct((n, d), BF16)],
        compiler_params=_cparams(("parallel",)),
    )(*mix, w_out, x, mod, g_post, g_pre)


def _ffn_kernel(h_ref, wgu_ref, wd_ref, x_ref, mod_ref, g_ref, o_ref, *, tf):
    f = pl.program_id(1)
    gu = _dot(h_ref[...], wgu_ref[...])
    gate, up = gu[:, :tf], gu[:, tf:]
    act = (gate / (1.0 + jnp.exp(-gate)) * up).astype(BF16)
    part = _dot(act, wd_ref[...])

    @pl.when(f == 0)
    def _():
        o_ref[...] = part

    @pl.when(f > 0)
    def _():
        o_ref[...] += part

    @pl.when(f == pl.num_programs(1) - 1)
    def _():
        o_ref[...] = x_ref[...] + mod_ref[5:6, :] * _rms(o_ref[...], g_ref[...])


def _ffn(h, wgu, wd, x, mod, g_post, tm, tf, mod_idx):
    n, d = x.shape
    d_ff = wd.shape[0]
    return pl.pallas_call(
        functools.partial(_ffn_kernel, tf=tf),
        grid=(n // tm, d_ff // tf),
        in_specs=[
            pl.BlockSpec((tm, d), lambda i, f: (i, 0)),
            pl.BlockSpec((d, 2 * tf), lambda i, f: (0, f)),
            pl.BlockSpec((tf, d), lambda i, f: (f, 0)),
            pl.BlockSpec((tm, d), lambda i, f: (i, 0)),
            pl.BlockSpec((None, 6, d), lambda i, f: (mod_idx(i), 0, 0)),
            pl.BlockSpec((1, d), lambda i, f: (0, 0)),
        ],
        out_specs=pl.BlockSpec((tm, d), lambda i, f: (i, 0)),
        out_shape=jax.ShapeDtypeStruct((n, d), F32),
        compiler_params=_cparams(("parallel", "arbitrary")),
    )(h, wgu, wd, x, mod, g_post)


def _rope_tables(n_tokens, dim):
    rows = n_tokens // GRID_W
    row = jnp.broadcast_to(jnp.arange(rows)[:, None], (rows, GRID_W)).reshape(-1).astype(F32)
    col = jnp.broadcast_to(jnp.arange(GRID_W)[None, :], (rows, GRID_W)).reshape(-1).astype(F32)
    quarter = dim // 4
    inv_freq = ROPE_BASE ** (-jnp.arange(quarter, dtype=F32) / quarter)
    ang = jnp.concatenate([row[:, None] * inv_freq, col[:, None] * inv_freq], axis=-1)
    cos, sin = jnp.cos(ang), jnp.sin(ang)
    reps = LANES // dim
    return (jnp.tile(jnp.concatenate([cos, cos], axis=-1), (1, reps)),
            jnp.tile(jnp.concatenate([-sin, sin], axis=-1), (1, reps)))


def _layer_weights(l, w_in, mla_w_uq, mla_w_ukv, w_out, ffn_w_gate, ffn_w_up, ffn_w_down, tf):
    d = w_in.shape[1]
    d_ff = ffn_w_gate.shape[-1]
    w_uq = mla_w_uq[l].reshape(MLA_Q_RANK, MLA_HEADS, MLA_QK_DIM)
    w_uq = jnp.pad(w_uq, ((0, 0), (0, 0), (0, MLA_QK_PAD - MLA_QK_DIM)))
    w_ukv = mla_w_ukv[l].reshape(MLA_KV_RANK, MLA_HEADS, 2, MLA_NOPE).transpose(0, 2, 1, 3)
    wgu = jnp.stack([ffn_w_gate[l].reshape(d, d_ff // tf, tf), ffn_w_up[l].reshape(d, d_ff // tf, tf)],
                    axis=2)
    return {
        'w_in': jnp.pad(w_in[l], ((0, 0), (0, P_COLS - IN_COLS))).astype(BF16),
        'w_uq': w_uq.reshape(MLA_Q_RANK, MLA_HEADS * MLA_QK_PAD).astype(BF16),
        'w_ukv': w_ukv.reshape(MLA_KV_RANK, MLA_HEADS * (MLA_NOPE + MLA_V)).astype(BF16),
        'w_out': w_out[l].astype(BF16),
        'wgu': wgu.reshape(d, 2 * d_ff).astype(BF16),
        'wd': ffn_w_down[l].astype(BF16),
    }


def _mixers(lp, a_q, m_q, a_lat, m_lat, a_ctx, m_ctx, batch, q_is_ctx, tq, lam_init):
    gq = _flash(a_q, A_GQ_Q, GQA_HEADS // GQA_KV_HEADS, a_lat, a_ctx, A_GQ_K, a_lat, a_ctx, A_GQ_V,
                GQA_KV_HEADS, 2 * HEAD_DIM, batch, q_is_ctx, tq)
    df = _flash(a_q, A_DF_Q, 2, a_lat, a_ctx, A_DF_K, a_lat, a_ctx, A_DF_V,
                DIFF_HEADS, HEAD_DIM, batch, q_is_ctx, tq, mode='diff',
                diff_params=lp['diff'], lam_init=lam_init)
    wn = _window(lp['win_sinks'], a_q, a_lat, a_ctx, batch, q_is_ctx, tq)
    ml = _flash(m_q, M_Q, 1, m_lat, m_ctx, M_K, a_lat, a_ctx, A_ML_V,
                MLA_HEADS, MLA_V, batch, q_is_ctx, tq)
    return gq, df, wn, ml


def kernel(x, c, ctx, c_ctx, ada_w, ada_b, norm_pre_mix, norm_post_mix, norm_pre_ffn, norm_post_ffn,
           w_in, gqa_q_norm, gqa_k_norm, diff_lambda_q1, diff_lambda_k1, diff_lambda_q2, diff_lambda_k2,
           diff_subln, win_sinks, mla_q_norm, mla_w_uq, mla_kv_norm, mla_w_ukv, w_out,
           ffn_w_gate, ffn_w_up, ffn_w_down):
    batch, t_lat, d = x.shape
    t_ctx = ctx.shape[1]
    n_layers = ada_w.shape[0]
    d_ff = ffn_w_gate.shape[-1]
    assert w_in.shape[-1] == IN_COLS and t_lat % GRID_W == 0

    tm_lat = min(1024, t_lat)
    tm_ctx = min(1024, batch * t_ctx)
    tp_lat = min(256, t_lat)
    tp_ctx = min(256, t_ctx)
    to_lat = min(256, t_lat)
    to_ctx = min(256, t_ctx)
    tff_lat = min(512, t_lat)
    tff_ctx = min(512, batch * t_ctx)
    tf = 512 if d_ff % 512 == 0 else 256
    tq_lat = min(512, t_lat)
    tq_ctx = t_ctx

    mod_rows = 16
    cc = jnp.zeros((mod_rows, d), F32).at[:batch].set(c).at[batch].set(c_ctx)
    mod_all = _modulation(cc, ada_w, ada_b).reshape(n_layers, mod_rows, 6, d)

    tables = _rope_tables(t_lat, HEAD_DIM) + _rope_tables(t_lat, DIFF_QK_DIM)

    x_lat = x.reshape(batch * t_lat, d)
    x_ctx = ctx.reshape(batch * t_ctx, d)
    ctx_mod = lambda i: batch

    row2 = lambda a: a.reshape(1, -1)
    for l in range(n_layers):
        last = l == n_layers - 1
        lam_init = 0.8 - 0.6 * math.exp(-0.3 * l)
        lp = _layer_weights(l, w_in, mla_w_uq, mla_w_ukv, w_out, ffn_w_gate, ffn_w_up, ffn_w_down, tf)
        lp.update({
            'gqa_q_norm': row2(gqa_q_norm[l]), 'gqa_k_norm': row2(gqa_k_norm[l]),
            'mla_q_norm': row2(mla_q_norm[l]), 'mla_kv_norm': row2(mla_kv_norm[l]),
            'diff': (row2(diff_lambda_q1[l]), row2(diff_lambda_k1[l]), row2(diff_lambda_q2[l]),
                     row2(diff_lambda_k2[l]), row2(diff_subln[l])),
            'win_sinks': win_sinks[l],
        })
        mod = mod_all[l]
        g_pre_mix, g_post_mix = row2(norm_pre_mix[l]), row2(norm_post_mix[l])
        g_pre_ffn, g_post_ffn = row2(norm_pre_ffn[l]), row2(norm_post_ffn[l])

        def lat_mod(tile):
            return lambda i: (i * tile) // t_lat

        p_lat = _inproj(x_lat, mod, g_pre_mix, lp['w_in'], tm_lat, lat_mod(tm_lat))
        p_ctx = _inproj(x_ctx, mod, g_pre_mix, lp['w_in'], tm_ctx, ctx_mod)
        a_lat, m_lat = _prep(p_lat, tables, lp, tp_lat, t_lat // tp_lat)
        a_ctx, m_ctx = _prep(p_ctx, None, lp, tp_ctx, 1)

        mix = _mixers(lp, a_lat, m_lat, a_lat, m_lat, a_ctx, m_ctx, batch, False, tq_lat, lam_init)
        x_mid, h2 = _outproj(mix, lp['w_out'], x_lat, mod, g_post_mix, g_pre_ffn, to_lat, lat_mod(to_lat))
        x_lat = _ffn(h2, lp['wgu'], lp['wd'], x_mid, mod, g_post_ffn, tff_lat, tf, lat_mod(tff_lat))

        if not last:
            mix_c = _mixers(lp, a_ctx, m_ctx, None, None, a_ctx, m_ctx, batch, True, tq_ctx, lam_init)
            xc_mid, hc2 = _outproj(mix_c, lp['w_out'], x_ctx, mod, g_post_mix, g_pre_ffn, to_ctx, ctx_mod)
            x_ctx = _ffn(hc2, lp['wgu'], lp['wd'], xc_mid, mod, g_post_ffn, tff_ctx, tf, ctx_mod)

    return x_lat.reshape(batch, t_lat, d)
```

```python
import functools
import math

import jax
import jax.numpy as jnp
from jax import lax
from jax.experimental import pallas as pl
from jax.experimental.pallas import tpu as pltpu

F32 = jnp.float32
BF16 = jnp.bfloat16

GRID_W = 64
HEAD_DIM = 128
GQA_HEADS, GQA_KV_HEADS = 4, 2
DIFF_HEADS, DIFF_QK_DIM = 4, 64
WIN_HEADS, WIN_KV_HEADS, WINDOW = 4, 2, 128
MLA_HEADS, MLA_Q_RANK, MLA_KV_RANK = 4, 512, 256
MLA_NOPE, MLA_ROPE, MLA_V = 128, 64, 128
MLA_QK_DIM = MLA_NOPE + MLA_ROPE
MLA_QK_PAD = 256
GROUP_WIDTH = 512
ROPE_BASE = 10000.0
NORM_EPS = 1e-6
NEG_INF = -1e30
LOG2E = math.log2(math.e)
LANES = 128

IN_COLS = 4416
P_COLS = 4608
C_GQ_Q, C_GQ_K, C_GQ_V = 0, 512, 768
C_DF_Q, C_DF_K, C_DF_V = 1024, 1536, 2048
C_WN_Q, C_WN_K, C_WN_V = 2560, 3072, 3328
C_ML_CQ, C_ML_CKV, C_ML_KR = 3584, 4096, 4352

A_GQ_Q, A_GQ_K = 0, 4
A_DF_Q, A_DF_K = 6, 14
A_WN_Q, A_WN_K, A_WN_V = 18, 22, 24
A_SLOTS = 26
VT_GQ, VT_DF, VT_ML = 0, 2, 6
VT_SLOTS = 10
KEY_CHUNK = 256
Q_SUB = 256
FLASH_ROWS = 2048
QK_LOOKAHEAD = 8
M_Q, M_K = 0, 4
M_SLOTS = 8

QS_HD = HEAD_DIM ** -0.5 * LOG2E
QS_DIFF = DIFF_QK_DIM ** -0.5 * LOG2E
QS_MLA = MLA_QK_DIM ** -0.5 * LOG2E

VMEM_LIMIT = 56 * 1024 * 1024


def _cparams(sem):
    return pltpu.CompilerParams(dimension_semantics=sem, vmem_limit_bytes=VMEM_LIMIT)


def _rms(x, g):
    return x * lax.rsqrt(jnp.mean(x * x, axis=-1, keepdims=True) + NORM_EPS) * g


def _dot(a, b):
    return jnp.dot(a, b, preferred_element_type=F32)


def _dot_nt(a, b):
    return lax.dot_general(a, b, (((1,), (1,)), ((), ())), preferred_element_type=F32)


def _mod_kernel(c_ref, w_ref, b_ref, o_ref):
    c = c_ref[...]
    s = (c / (1.0 + jnp.exp(-c))).astype(BF16)
    o_ref[...] = _dot(s, w_ref[...].astype(BF16)) + b_ref[...]


def _modulation(cc, ada_w, ada_b):
    n_layers, d, d6 = ada_w.shape
    rows = cc.shape[0]
    tn = d // 2
    return pl.pallas_call(
        _mod_kernel,
        grid=(n_layers, d6 // tn),
        in_specs=[
            pl.BlockSpec((rows, d), lambda l, j: (0, 0)),
            pl.BlockSpec((None, d, tn), lambda l, j: (l, 0, j)),
            pl.BlockSpec((None, 1, tn), lambda l, j: (l, 0, j)),
        ],
        out_specs=pl.BlockSpec((None, rows, tn), lambda l, j: (l, 0, j)),
        out_shape=jax.ShapeDtypeStruct((n_layers, rows, d6), F32),
        compiler_params=_cparams(("parallel", "parallel")),
    )(cc, ada_w, ada_b.reshape(n_layers, 1, d6))


def _inproj_kernel(x_ref, mod_ref, g_ref, w_ref, o_ref, h_scr):
    @pl.when(pl.program_id(1) == 0)
    def _():
        h = _rms(x_ref[...], g_ref[...])
        h_scr[...] = (h * (1.0 + mod_ref[1:2, :]) + mod_ref[0:1, :]).astype(BF16)

    o_ref[...] = _dot(h_scr[...], w_ref[...]).astype(BF16)


def _inproj(x, mod, gain, w, tm, mod_idx):
    n, d = x.shape
    tn = 512
    return pl.pallas_call(
        _inproj_kernel,
        grid=(n // tm, P_COLS // tn),
        in_specs=[
            pl.BlockSpec((tm, d), lambda i, j: (i, 0)),
            pl.BlockSpec((None, 6, d), lambda i, j: (mod_idx(i), 0, 0)),
            pl.BlockSpec((1, d), lambda i, j: (0, 0)),
            pl.BlockSpec((d, tn), lambda i, j: (0, j)),
        ],
        out_specs=pl.BlockSpec((tm, tn), lambda i, j: (i, j)),
        out_shape=jax.ShapeDtypeStruct((n, P_COLS), BF16),
        scratch_shapes=[pltpu.VMEM((tm, d), BF16)],
        compiler_params=_cparams(("parallel", "arbitrary")),
    )(x, mod, gain, w)


def _prep_kernel(*refs, rotate):
    if rotate:
        p_ref, c128_ref, s128_ref, c64_ref, s64_ref = refs[:5]
        refs = refs[5:]
    else:
        p_ref = refs[0]
        refs = refs[1:]
    gq_ref, gk_ref, qn_ref, wuq_ref, kvn_ref, wukv_ref, a_ref, m_ref, vt_ref = refs
    tp = p_ref.shape[0]

    def cols(lo, width=LANES):
        return p_ref[:, lo:lo + width].astype(F32)

    if rotate:
        lane = lax.broadcasted_iota(jnp.int32, (tp, LANES), 1)
        low32 = (lane & 63) < 32

        def rope128(x):
            return x * c128_ref[...] + pltpu.roll(x, 64, 1) * s128_ref[...]

        def rope64(x):
            rot = jnp.where(low32, pltpu.roll(x, 96, 1), pltpu.roll(x, 32, 1))
            return x * c64_ref[...] + rot * s64_ref[...]
    else:
        rope128 = rope64 = lambda x: x

    first_half = lax.broadcasted_iota(jnp.int32, (tp, LANES), 1) < 64

    for h in range(GQA_HEADS):
        x = rope128(_rms(cols(C_GQ_Q + LANES * h), gq_ref[...]))
        a_ref[A_GQ_Q + h] = (x * QS_HD).astype(BF16)
    for g in range(GQA_KV_HEADS):
        x = rope128(_rms(cols(C_GQ_K + LANES * g), gk_ref[...]))
        a_ref[A_GQ_K + g] = x.astype(BF16)
        vt_ref[VT_GQ + g] = cols(C_GQ_V + LANES * g).T.astype(BF16)

    for h in range(DIFF_HEADS):
        x = rope64(cols(C_DF_Q + LANES * h)) * QS_DIFF
        a_ref[A_DF_Q + 2 * h] = jnp.where(first_half, x, 0.0).astype(BF16)
        a_ref[A_DF_Q + 2 * h + 1] = jnp.where(first_half, 0.0, x).astype(BF16)
        a_ref[A_DF_K + h] = rope64(cols(C_DF_K + LANES * h)).astype(BF16)
        vt_ref[VT_DF + h] = cols(C_DF_V + LANES * h).T.astype(BF16)

    for h in range(WIN_HEADS):
        a_ref[A_WN_Q + h] = (rope128(cols(C_WN_Q + LANES * h)) * QS_HD).astype(BF16)
    for g in range(WIN_KV_HEADS):
        a_ref[A_WN_K + g] = rope128(cols(C_WN_K + LANES * g)).astype(BF16)
        a_ref[A_WN_V + g] = p_ref[:, C_WN_V + LANES * g:C_WN_V + LANES * (g + 1)]

    hq = _rms(cols(C_ML_CQ, MLA_Q_RANK), qn_ref[...]).astype(BF16)
    qa = _dot(hq, wuq_ref[...])
    hkv = _rms(cols(C_ML_CKV, MLA_KV_RANK), kvn_ref[...]).astype(BF16)
    kva = _dot(hkv, wukv_ref[...])
    kr = rope64(cols(C_ML_KR)).astype(BF16)
    for h in range(MLA_HEADS):
        lo = MLA_QK_PAD * h
        m_ref[M_Q + h, :, 0:LANES] = (qa[:, lo:lo + LANES] * QS_MLA).astype(BF16)
        m_ref[M_Q + h, :, LANES:2 * LANES] = (
            rope64(qa[:, lo + LANES:lo + 2 * LANES]) * QS_MLA).astype(BF16)
        m_ref[M_K + h, :, 0:LANES] = kva[:, LANES * h:LANES * (h + 1)].astype(BF16)
        m_ref[M_K + h, :, LANES:2 * LANES] = kr
        v_lo = MLA_HEADS * MLA_NOPE + MLA_V * h
        vt_ref[VT_ML + h] = kva[:, v_lo:v_lo + MLA_V].T.astype(BF16)


def _prep(p, tables, lp, tp, tiles_per_seq):
    n = p.shape[0]
    rotate = tables is not None
    const = lambda i: (0, 0)
    in_specs = [pl.BlockSpec((tp, P_COLS), lambda i: (i, 0))]
    args = [p]
    if rotate:
        tab = pl.BlockSpec((tp, LANES), lambda i: (i % tiles_per_seq, 0))
        in_specs += [tab] * 4
        args += list(tables)
    in_specs += [
        pl.BlockSpec((1, HEAD_DIM), const),
        pl.BlockSpec((1, HEAD_DIM), const),
        pl.BlockSpec((1, MLA_Q_RANK), const),
        pl.BlockSpec((MLA_Q_RANK, MLA_HEADS * MLA_QK_PAD), const),
        pl.BlockSpec((1, MLA_KV_RANK), const),
        pl.BlockSpec((MLA_KV_RANK, MLA_HEADS * (MLA_NOPE + MLA_V)), const),
    ]
    args += [lp['gqa_q_norm'], lp['gqa_k_norm'], lp['mla_q_norm'], lp['w_uq'],
             lp['mla_kv_norm'], lp['w_ukv']]
    return pl.pallas_call(
        functools.partial(_prep_kernel, rotate=rotate),
        grid=(n // tp,),
        in_specs=in_specs,
        out_specs=[
            pl.BlockSpec((A_SLOTS, tp, LANES), lambda i: (0, i, 0)),
            pl.BlockSpec((M_SLOTS, tp, MLA_QK_PAD), lambda i: (0, i, 0)),
            pl.BlockSpec((VT_SLOTS, None, LANES, tp), lambda i: (0, i, 0, 0)),
        ],
        out_shape=[
            jax.ShapeDtypeStruct((A_SLOTS, n, LANES), BF16),
            jax.ShapeDtypeStruct((M_SLOTS, n, MLA_QK_PAD), BF16),
            jax.ShapeDtypeStruct((VT_SLOTS, n // tp, LANES, tp), BF16),
        ],
        compiler_params=_cparams(("parallel",)),
    )(*args)


def _flash_kernel(*refs, mode, has_latent, tk, lam_init):
    q_ref, kc_ref, vct_ref = refs[:3]
    refs = refs[3:]
    if has_latent:
        kl_ref, vlt_ref = refs[:2]
        refs = refs[2:]
    if mode == 'diff':
        lq1_ref, lk1_ref, lq2_ref, lk2_ref, sub_ref = refs[:5]
        refs = refs[5:]
    o_ref, m_scr, l_scr, acc_scr = refs
    n_rep, tq, dk = q_ref.shape
    dv = vct_ref.shape[-2]
    sub_per_rep = tq // Q_SUB
    n_sub = n_rep * sub_per_rep

    def q_sub(j):
        r, s = divmod(j, sub_per_rep)
        return q_ref[r, Q_SUB * s:Q_SUB * (s + 1), :]

    def pv(vt_ref, first_chunk, n_chunks, p):
        out = None
        for c in range(n_chunks):
            part = _dot(vt_ref[first_chunk + c], p[KEY_CHUNK * c:KEY_CHUNK * (c + 1)])
            out = part if out is None else out + part
        return out

    def sweep(k, vt_ref, first_chunk, n_chunks, first):
        pending = [_dot_nt(k, q_sub(j)) for j in range(min(QK_LOOKAHEAD, n_sub))]
        for j in range(n_sub):
            st = pending.pop(0)
            if j + QK_LOOKAHEAD < n_sub:
                pending.append(_dot_nt(k, q_sub(j + QK_LOOKAHEAD)))
            m_new = jnp.max(st, axis=0, keepdims=True)
            if not first:
                m_old = m_scr[j]
                m_new = jnp.maximum(m_old, m_new)
                alpha = jnp.exp2(m_old - m_new)
            p = jnp.exp2(st - m_new)
            l_new = jnp.sum(p, axis=0, keepdims=True)
            acc_new = pv(vt_ref, first_chunk, n_chunks, p.astype(BF16))
            m_scr[j] = m_new
            l_scr[j] = l_new if first else alpha * l_scr[j] + l_new
            acc_scr[j] = acc_new if first else alpha * acc_scr[j] + acc_new

    sweep(kc_ref[...], vct_ref, 0, kc_ref.shape[0] // KEY_CHUNK, True)

    if has_latent:
        chunks_per_step = tk // KEY_CHUNK

        def body(c, carry):
            k = kl_ref[pl.ds(pl.multiple_of(c * tk, tk), tk), :]
            sweep(k, vlt_ref, c * chunks_per_step, chunks_per_step, False)
            return carry

        lax.fori_loop(0, kl_ref.shape[0] // tk, body, 0)

    def out_t(j):
        return acc_scr[j] / l_scr[j]

    if mode == 'plain':
        for j in range(n_sub):
            r, s = divmod(j, sub_per_rep)
            o_ref[Q_SUB * s:Q_SUB * (s + 1), dv * r:dv * (r + 1)] = out_t(j).T.astype(o_ref.dtype)
    else:
        lam = (jnp.exp(jnp.sum(lq1_ref[...] * lk1_ref[...], keepdims=True))
               - jnp.exp(jnp.sum(lq2_ref[...] * lk2_ref[...], keepdims=True)) + lam_init)
        for s in range(sub_per_rep):
            d = (out_t(s) - lam * out_t(sub_per_rep + s)).T
            o_ref[Q_SUB * s:Q_SUB * (s + 1), :] = (
                _rms(d, sub_ref[...]) * (1.0 - lam_init)).astype(o_ref.dtype)


def _flash(q_arr, q_slot, n_rep, k_lat, k_ctx, k_slot, vt_lat, vt_ctx, v_slot, n_kv, out_w,
           batch, q_is_ctx, tq, mode='plain', diff_params=None, lam_init=0.0):
    dk = q_arr.shape[-1]
    dv = vt_ctx.shape[-2]
    n_q = q_arr.shape[1]
    t_ctx = k_ctx.shape[1] // batch
    nq = n_q // batch // tq
    has_latent = not q_is_ctx
    q_blk = q_slot // n_rep
    assert tq % Q_SUB == 0 and t_ctx % KEY_CHUNK == 0
    in_specs = [
        pl.BlockSpec((n_rep, tq, dk), lambda b, g, i: (q_blk + g, b * nq + i, 0)),
        pl.BlockSpec((None, t_ctx, dk), lambda b, g, i: (k_slot + g, b, 0)),
        pl.BlockSpec((None, t_ctx // KEY_CHUNK, dv, KEY_CHUNK), lambda b, g, i: (v_slot + g, b, 0, 0)),
    ]
    args = [q_arr, k_ctx, vt_ctx]
    tk = 0
    if has_latent:
        t_lat = k_lat.shape[1] // batch
        tk = min(512, t_lat)
        assert t_lat % tk == 0 and tk % KEY_CHUNK == 0
        in_specs += [
            pl.BlockSpec((None, t_lat, dk), lambda b, g, i: (k_slot + g, b, 0)),
            pl.BlockSpec((None, t_lat // KEY_CHUNK, dv, KEY_CHUNK),
                         lambda b, g, i: (v_slot + g, b, 0, 0)),
        ]
        args += [k_lat, vt_lat]
    if mode == 'diff':
        in_specs += [pl.BlockSpec((1, DIFF_QK_DIM), lambda b, g, i: (0, 0))] * 4
        in_specs += [pl.BlockSpec((1, dv), lambda b, g, i: (0, 0))]
        args += list(diff_params)
    n_sub = n_rep * tq // Q_SUB
    return pl.pallas_call(
        functools.partial(_flash_kernel, mode=mode, has_latent=has_latent, tk=tk, lam_init=lam_init),
        grid=(batch, n_kv, nq),
        in_specs=in_specs,
        out_specs=pl.BlockSpec((tq, out_w), lambda b, g, i: (b * nq + i, g)),
        out_shape=jax.ShapeDtypeStruct((n_q, n_kv * out_w), BF16),
        scratch_shapes=[pltpu.VMEM((n_sub, 1, Q_SUB), F32), pltpu.VMEM((n_sub, 1, Q_SUB), F32),
                        pltpu.VMEM((n_sub, dv, Q_SUB), F32)],
        compiler_params=_cparams(("parallel", "parallel", "arbitrary")),
    )(*args)


def _win_kernel(*refs, has_band, band_w, t_lat):
    sink_ref, q_ref, kc_ref, vc_ref = refs[:4]
    refs = refs[4:]
    if has_band:
        kl_ref, vl_ref = refs[:2]
        refs = refs[2:]
    (o_ref,) = refs
    n_rep, tq, dk = q_ref.shape
    dv = vc_ref.shape[-1]
    rows = n_rep * tq
    g = pl.program_id(1)
    i = pl.program_id(2)
    q = q_ref[...].reshape(rows, dk)

    row = lax.broadcasted_iota(jnp.int32, (rows, 1), 0)
    sink = jnp.zeros((rows, 1), F32)
    for r in range(n_rep):
        in_r = (row >= tq * r) & (row < tq * (r + 1))
        sink = jnp.where(in_r, sink_ref[g * n_rep + r] * LOG2E, sink)

    s_c = _dot_nt(q, kc_ref[...])
    m = jnp.maximum(jnp.max(s_c, axis=-1, keepdims=True), sink)
    if has_band:
        start = pl.multiple_of(jnp.clip(i * tq - WINDOW, 0, t_lat - band_w), LANES)
        s_b = _dot_nt(q, kl_ref[pl.ds(start, band_w), :])
        q_pos = i * tq + (lax.broadcasted_iota(jnp.int32, (rows, band_w), 0) & (tq - 1))
        k_pos = start + lax.broadcasted_iota(jnp.int32, (rows, band_w), 1)
        s_b = jnp.where(jnp.abs(q_pos - k_pos) <= WINDOW, s_b, NEG_INF)
        m = jnp.maximum(m, jnp.max(s_b, axis=-1, keepdims=True))
    p_c = jnp.exp2(s_c - m)
    l = jnp.sum(p_c, axis=-1, keepdims=True) + jnp.exp2(sink - m)
    acc = _dot(p_c.astype(BF16), vc_ref[...])
    if has_band:
        p_b = jnp.exp2(s_b - m)
        l = l + jnp.sum(p_b, axis=-1, keepdims=True)
        acc = acc + _dot(p_b.astype(BF16), vl_ref[pl.ds(start, band_w), :])
    o = acc / l
    for r in range(n_rep):
        o_ref[:, dv * r:dv * (r + 1)] = o[tq * r:tq * (r + 1)].astype(o_ref.dtype)


def _window(sinks, a_q, a_lat, a_ctx, batch, q_is_ctx, tq):
    n_rep = WIN_HEADS // WIN_KV_HEADS
    n_q = a_q.shape[1]
    t_ctx = a_ctx.shape[1] // batch
    nq = n_q // batch // tq
    has_band = not q_is_ctx
    q_blk = A_WN_Q // n_rep
    in_specs = [
        pl.BlockSpec(memory_space=pltpu.SMEM),
        pl.BlockSpec((n_rep, tq, HEAD_DIM), lambda b, g, i: (q_blk + g, b * nq + i, 0)),
        pl.BlockSpec((None, t_ctx, HEAD_DIM), lambda b, g, i: (A_WN_K + g, b, 0)),
        pl.BlockSpec((None, t_ctx, HEAD_DIM), lambda b, g, i: (A_WN_V + g, b, 0)),
    ]
    args = [sinks, a_q, a_ctx, a_ctx]
    band_w = t_lat = 0
    if has_band:
        t_lat = a_lat.shape[1] // batch
        band_w = min(tq + 2 * WINDOW, t_lat)
        assert tq & (tq - 1) == 0
        in_specs += [
            pl.BlockSpec((None, t_lat, HEAD_DIM), lambda b, g, i: (A_WN_K + g, b, 0)),
            pl.BlockSpec((None, t_lat, HEAD_DIM), lambda b, g, i: (A_WN_V + g, b, 0)),
        ]
        args += [a_lat, a_lat]
    return pl.pallas_call(
        functools.partial(_win_kernel, has_band=has_band, band_w=band_w, t_lat=t_lat),
        grid=(batch, WIN_KV_HEADS, nq),
        in_specs=in_specs,
        out_specs=pl.BlockSpec((tq, n_rep * HEAD_DIM), lambda b, g, i: (b * nq + i, g)),
        out_shape=jax.ShapeDtypeStruct((n_q, GROUP_WIDTH), BF16),
        compiler_params=_cparams(("parallel", "parallel", "arbitrary")),
    )(*args)


def _outproj_kernel(a0_ref, a1_ref, a2_ref, a3_ref, w_ref, x_ref, mod_ref, gpost_ref, gpre_ref,
                    xo_ref, h_ref):
    y = None
    for k, a_ref in enumerate((a0_ref, a1_ref, a2_ref, a3_ref)):
        part = _dot(a_ref[...], w_ref[GROUP_WIDTH * k:GROUP_WIDTH * (k + 1), :])
        y = part if y is None else y + part
    xm = x_ref[...] + mod_ref[2:3, :] * _rms(y, gpost_ref[...])
    xo_ref[...] = xm
    h_ref[...] = (_rms(xm, gpre_ref[...]) * (1.0 + mod_ref[4:5, :]) + mod_ref[3:4, :]).astype(BF16)


def _outproj(mix, w_out, x, mod, g_post, g_pre, tm, mod_idx):
    n, d = x.shape
    const = lambda i: (0, 0)
    return pl.pallas_call(
        _outproj_kernel,
        grid=(n // tm,),
        in_specs=[pl.BlockSpec((tm, GROUP_WIDTH), lambda i: (i, 0))] * 4 + [
            pl.BlockSpec(w_out.shape, const),
            pl.BlockSpec((tm, d), lambda i: (i, 0)),
            pl.BlockSpec((None, 6, d), lambda i: (mod_idx(i), 0, 0)),
            pl.BlockSpec((1, d), const),
            pl.BlockSpec((1, d), const),
        ],
        out_specs=[pl.BlockSpec((tm, d), lambda i: (i, 0)), pl.BlockSpec((tm, d), lambda i: (i, 0))],
        out_shape=[jax.ShapeDtypeStruct((n, d), F32), jax.ShapeDtypeStruct((n, d), BF16)],
        compiler_params=_cparams(("parallel",)),
    )(*mix, w_out, x, mod, g_post, g_pre)


def _ffn_kernel(h_ref, wgu_ref, wd_ref, x_ref, mod_ref, g_ref, o_ref, *, tf):
    f = pl.program_id(1)
    gu = _dot(h_ref[...], wgu_ref[...])
    gate, up = gu[:, :tf], gu[:, tf:]
    act = (gate / (1.0 + jnp.exp(-gate)) * up).astype(BF16)
    part = _dot(act, wd_ref[...])

    @pl.when(f == 0)
    def _():
        o_ref[...] = part

    @pl.when(f > 0)
    def _():
        o_ref[...] += part

    @pl.when(f == pl.num_programs(1) - 1)
    def _():
        o_ref[...] = x_ref[...] + mod_ref[5:6, :] * _rms(o_ref[...], g_ref[...])


def _ffn(h, wgu, wd, x, mod, g_post, tm, tf, mod_idx):
    n, d = x.shape
    d_ff = wd.shape[0]
    return pl.pallas_call(
        functools.partial(_ffn_kernel, tf=tf),
        grid=(n // tm, d_ff // tf),
        in_specs=[
            pl.BlockSpec((tm, d), lambda i, f: (i, 0)),
            pl.BlockSpec((d, 2 * tf), lambda i, f: (0, f)),
            pl.BlockSpec((tf, d), lambda i, f: (f, 0)),
            pl.BlockSpec((tm, d), lambda i, f: (i, 0)),
            pl.BlockSpec((None, 6, d), lambda i, f: (mod_idx(i), 0, 0)),
            pl.BlockSpec((1, d), lambda i, f: (0, 0)),
        ],
        out_specs=pl.BlockSpec((tm, d), lambda i, f: (i, 0)),
        out_shape=jax.ShapeDtypeStruct((n, d), F32),
        compiler_params=_cparams(("parallel", "arbitrary")),
    )(h, wgu, wd, x, mod, g_post)


def _rope_tables(n_tokens, dim):
    rows = n_tokens // GRID_W
    row = jnp.broadcast_to(jnp.arange(rows)[:, None], (rows, GRID_W)).reshape(-1).astype(F32)
    col = jnp.broadcast_to(jnp.arange(GRID_W)[None, :], (rows, GRID_W)).reshape(-1).astype(F32)
    quarter = dim // 4
    inv_freq = ROPE_BASE ** (-jnp.arange(quarter, dtype=F32) / quarter)
    ang = jnp.concatenate([row[:, None] * inv_freq, col[:, None] * inv_freq], axis=-1)
    cos, sin = jnp.cos(ang), jnp.sin(ang)
    reps = LANES // dim
    return (jnp.tile(jnp.concatenate([cos, cos], axis=-1), (1, reps)),
            jnp.tile(jnp.concatenate([-sin, sin], axis=-1), (1, reps)))


def _layer_weights(l, w_in, mla_w_uq, mla_w_ukv, w_out, ffn_w_gate, ffn_w_up, ffn_w_down, tf):
    d = w_in.shape[1]
    d_ff = ffn_w_gate.shape[-1]
    w_uq = mla_w_uq[l].reshape(MLA_Q_RANK, MLA_HEADS, MLA_QK_DIM)
    w_uq = jnp.pad(w_uq, ((0, 0), (0, 0), (0, MLA_QK_PAD - MLA_QK_DIM)))
    w_ukv = mla_w_ukv[l].reshape(MLA_KV_RANK, MLA_HEADS, 2, MLA_NOPE).transpose(0, 2, 1, 3)
    wgu = jnp.stack([ffn_w_gate[l].reshape(d, d_ff // tf, tf), ffn_w_up[l].reshape(d, d_ff // tf, tf)],
                    axis=2)
    return {
        'w_in': jnp.pad(w_in[l], ((0, 0), (0, P_COLS - IN_COLS))).astype(BF16),
        'w_uq': w_uq.reshape(MLA_Q_RANK, MLA_HEADS * MLA_QK_PAD).astype(BF16),
        'w_ukv': w_ukv.reshape(MLA_KV_RANK, MLA_HEADS * (MLA_NOPE + MLA_V)).astype(BF16),
        'w_out': w_out[l].astype(BF16),
        'wgu': wgu.reshape(d, 2 * d_ff).astype(BF16),
        'wd': ffn_w_down[l].astype(BF16),
    }


def _mixers(lp, prep_q, prep_lat, prep_ctx, batch, q_is_ctx, t_q, lam_init):
    a_q, m_q, _ = prep_q
    a_lat, m_lat, vt_lat = prep_lat if prep_lat is not None else (None, None, None)
    a_ctx, m_ctx, vt_ctx = prep_ctx
    tq = lambda n_rep: min(FLASH_ROWS // n_rep, t_q)
    gq = _flash(a_q, A_GQ_Q, GQA_HEADS // GQA_KV_HEADS, a_lat, a_ctx, A_GQ_K, vt_lat, vt_ctx, VT_GQ,
                GQA_KV_HEADS, 2 * HEAD_DIM, batch, q_is_ctx, tq(2))
    df = _flash(a_q, A_DF_Q, 2, a_lat, a_ctx, A_DF_K, vt_lat, vt_ctx, VT_DF,
                DIFF_HEADS, HEAD_DIM, batch, q_is_ctx, tq(2), mode='diff',
                diff_params=lp['diff'], lam_init=lam_init)
    wn = _window(lp['win_sinks'], a_q, a_lat, a_ctx, batch, q_is_ctx, min(512, t_q))
    ml = _flash(m_q, M_Q, 1, m_lat, m_ctx, M_K, vt_lat, vt_ctx, VT_ML,
                MLA_HEADS, MLA_V, batch, q_is_ctx, tq(1))
    return gq, df, wn, ml


def kernel(x, c, ctx, c_ctx, ada_w, ada_b, norm_pre_mix, norm_post_mix, norm_pre_ffn, norm_post_ffn,
           w_in, gqa_q_norm, gqa_k_norm, diff_lambda_q1, diff_lambda_k1, diff_lambda_q2, diff_lambda_k2,
           diff_subln, win_sinks, mla_q_norm, mla_w_uq, mla_kv_norm, mla_w_ukv, w_out,
           ffn_w_gate, ffn_w_up, ffn_w_down):
    batch, t_lat, d = x.shape
    t_ctx = ctx.shape[1]
    n_layers = ada_w.shape[0]
    d_ff = ffn_w_gate.shape[-1]
    assert w_in.shape[-1] == IN_COLS and t_lat % GRID_W == 0

    tm_lat = min(1024, t_lat)
    tm_ctx = min(1024, batch * t_ctx)
    to_lat = min(256, t_lat)
    to_ctx = min(256, t_ctx)
    tff_lat = min(512, t_lat)
    tff_ctx = min(512, batch * t_ctx)
    tf = 512 if d_ff % 512 == 0 else 256

    mod_rows = 16
    cc = jnp.zeros((mod_rows, d), F32).at[:batch].set(c).at[batch].set(c_ctx)
    mod_all = _modulation(cc, ada_w, ada_b).reshape(n_layers, mod_rows, 6, d)

    tables = _rope_tables(t_lat, HEAD_DIM) + _rope_tables(t_lat, DIFF_QK_DIM)

    x_lat = x.reshape(batch * t_lat, d)
    x_ctx = ctx.reshape(batch * t_ctx, d)
    ctx_mod = lambda i: batch

    row2 = lambda a: a.reshape(1, -1)
    for l in range(n_layers):
        last = l == n_layers - 1
        lam_init = 0.8 - 0.6 * math.exp(-0.3 * l)
        lp = _layer_weights(l, w_in, mla_w_uq, mla_w_ukv, w_out, ffn_w_gate, ffn_w_up, ffn_w_down, tf)
        lp.update({
            'gqa_q_norm': row2(gqa_q_norm[l]), 'gqa_k_norm': row2(gqa_k_norm[l]),
            'mla_q_norm': row2(mla_q_norm[l]), 'mla_kv_norm': row2(mla_kv_norm[l]),
            'diff': (row2(diff_lambda_q1[l]), row2(diff_lambda_k1[l]), row2(diff_lambda_q2[l]),
                     row2(diff_lambda_k2[l]), row2(diff_subln[l])),
            'win_sinks': win_sinks[l],
        })
        mod = mod_all[l]
        g_pre_mix, g_post_mix = row2(norm_pre_mix[l]), row2(norm_post_mix[l])
        g_pre_ffn, g_post_ffn = row2(norm_pre_ffn[l]), row2(norm_post_ffn[l])

        def lat_mod(tile):
            return lambda i: (i * tile) // t_lat

        p_lat = _inproj(x_lat, mod, g_pre_mix, lp['w_in'], tm_lat, lat_mod(tm_lat))
        p_ctx = _inproj(x_ctx, mod, g_pre_mix, lp['w_in'], tm_ctx, ctx_mod)
        prep_lat = _prep(p_lat, tables, lp, KEY_CHUNK, t_lat // KEY_CHUNK)
        prep_ctx = _prep(p_ctx, None, lp, KEY_CHUNK, 1)

        mix = _mixers(lp, prep_lat, prep_lat, prep_ctx, batch, False, t_lat, lam_init)
        x_mid, h2 = _outproj(mix, lp['w_out'], x_lat, mod, g_post_mix, g_pre_ffn, to_lat, lat_mod(to_lat))
        x_lat = _ffn(h2, lp['wgu'], lp['wd'], x_mid, mod, g_post_ffn, tff_lat, tf, lat_mod(tff_lat))

        if not last:
            mix_c = _mixers(lp, prep_ctx, None, prep_ctx, batch, True, t_ctx, lam_init)
            xc_mid, hc2 = _outproj(mix_c, lp['w_out'], x_ctx, mod, g_post_mix, g_pre_ffn, to_ctx, ctx_mod)
            x_ctx = _ffn(hc2, lp['wgu'], lp['wd'], xc_mid, mod, g_post_ffn, tff_ctx, tf, ctx_mod)

    return x_lat.reshape(batch, t_lat, d)
```

```python
import functools
import math

import jax
import jax.numpy as jnp
from jax import lax
from jax.experimental import pallas as pl
from jax.experimental.pallas import tpu as pltpu

F32 = jnp.float32
BF16 = jnp.bfloat16

GRID_W = 64
HEAD_DIM = 128
GQA_HEADS, GQA_KV_HEADS = 4, 2
DIFF_HEADS, DIFF_QK_DIM = 4, 64
WIN_HEADS, WIN_KV_HEADS, WINDOW = 4, 2, 128
MLA_HEADS, MLA_Q_RANK, MLA_KV_RANK = 4, 512, 256
MLA_NOPE, MLA_ROPE, MLA_V = 128, 64, 128
MLA_QK_DIM = MLA_NOPE + MLA_ROPE
MLA_QK_PAD = 256
GROUP_WIDTH = 512
ROPE_BASE = 10000.0
NORM_EPS = 1e-6
NEG_INF = -1e30
LOG2E = math.log2(math.e)
LANES = 128

IN_COLS = 4416
P_COLS = 4608
C_GQ_Q, C_GQ_K, C_GQ_V = 0, 512, 768
C_DF_Q, C_DF_K, C_DF_V = 1024, 1536, 2048
C_WN_Q, C_WN_K, C_WN_V = 2560, 3072, 3328
C_ML_CQ, C_ML_CKV, C_ML_KR = 3584, 4096, 4352

A_GQ_Q, A_GQ_K = 0, 4
A_DF_Q, A_DF_K = 6, 14
A_WN_Q, A_WN_K, A_WN_V = 18, 22, 24
A_SLOTS = 26
VT_GQ, VT_DF, VT_ML = 0, 2, 6
VT_SLOTS = 10
KEY_CHUNK = 512
Q_SUB = 256
FLASH_ROWS = 2048
QK_LOOKAHEAD = 8
INPROJ_PIECES = 4
OUTPROJ_PIECES = 2
FLASH_CHUNKS_PER_ITER = 2
FFN_PIECES = 2
M_Q, M_K = 0, 4
M_SLOTS = 8

QS_HD = HEAD_DIM ** -0.5 * LOG2E
QS_DIFF = DIFF_QK_DIM ** -0.5 * LOG2E
QS_MLA = MLA_QK_DIM ** -0.5 * LOG2E

VMEM_LIMIT = 56 * 1024 * 1024


def _cparams(sem):
    return pltpu.CompilerParams(dimension_semantics=sem, vmem_limit_bytes=VMEM_LIMIT)


def _rms(x, g):
    return x * lax.rsqrt(jnp.mean(x * x, axis=-1, keepdims=True) + NORM_EPS) * g


def _dot(a, b):
    return jnp.dot(a, b, preferred_element_type=F32)


def _dot_nt(a, b):
    return lax.dot_general(a, b, (((1,), (1,)), ((), ())), preferred_element_type=F32)


def _mod_kernel(c_ref, w_ref, b_ref, o_ref):
    c = c_ref[...]
    s = (c / (1.0 + jnp.exp(-c))).astype(BF16)
    o_ref[...] = _dot(s, w_ref[...].astype(BF16)) + b_ref[...]


def _modulation(cc, ada_w, ada_b):
    n_layers, d, d6 = ada_w.shape
    rows = cc.shape[0]
    tn = d // 2
    return pl.pallas_call(
        _mod_kernel,
        grid=(n_layers, d6 // tn),
        in_specs=[
            pl.BlockSpec((rows, d), lambda l, j: (0, 0)),
            pl.BlockSpec((None, d, tn), lambda l, j: (l, 0, j)),
            pl.BlockSpec((None, 1, tn), lambda l, j: (l, 0, j)),
        ],
        out_specs=pl.BlockSpec((None, rows, tn), lambda l, j: (l, 0, j)),
        out_shape=jax.ShapeDtypeStruct((n_layers, rows, d6), F32),
        compiler_params=_cparams(("parallel", "parallel")),
    )(cc, ada_w, ada_b.reshape(n_layers, 1, d6))


def _inproj_kernel(x_ref, mod_ref, g_ref, w_ref, o_ref, h_scr):
    j = pl.program_id(1)
    piece = x_ref.shape[0] // INPROJ_PIECES

    @pl.when(j == 0)
    def _():
        for p in range(INPROJ_PIECES):
            rows = slice(piece * p, piece * (p + 1))
            h = _rms(x_ref[rows, :], g_ref[...])
            h = (h * (1.0 + mod_ref[1:2, :]) + mod_ref[0:1, :]).astype(BF16)
            h_scr[rows, :] = h
            o_ref[rows, :] = _dot(h, w_ref[...]).astype(BF16)

    @pl.when(j > 0)
    def _():
        o_ref[...] = _dot(h_scr[...], w_ref[...]).astype(BF16)


def _inproj(x, mod, gain, w, tm, mod_idx):
    n, d = x.shape
    tn = 512
    return pl.pallas_call(
        _inproj_kernel,
        grid=(n // tm, P_COLS // tn),
        in_specs=[
            pl.BlockSpec((tm, d), lambda i, j: (i, 0)),
            pl.BlockSpec((None, 6, d), lambda i, j: (mod_idx(i), 0, 0)),
            pl.BlockSpec((1, d), lambda i, j: (0, 0)),
            pl.BlockSpec((d, tn), lambda i, j: (0, j)),
        ],
        out_specs=pl.BlockSpec((tm, tn), lambda i, j: (i, j)),
        out_shape=jax.ShapeDtypeStruct((n, P_COLS), BF16),
        scratch_shapes=[pltpu.VMEM((tm, d), BF16)],
        compiler_params=_cparams(("parallel", "arbitrary")),
    )(x, mod, gain, w)


def _prep_kernel(*refs, rotate):
    if rotate:
        p_ref, c128_ref, s128_ref, c64_ref, s64_ref = refs[:5]
        refs = refs[5:]
    else:
        p_ref = refs[0]
        refs = refs[1:]
    gq_ref, gk_ref, qn_ref, wuq_ref, kvn_ref, wukv_ref, a_ref, m_ref, vt_ref = refs
    tp = p_ref.shape[0]

    def cols(lo, width=LANES):
        return p_ref[:, lo:lo + width].astype(F32)

    if rotate:
        lane = lax.broadcasted_iota(jnp.int32, (tp, LANES), 1)
        low32 = (lane & 63) < 32

        def rope128(x):
            return x * c128_ref[...] + pltpu.roll(x, 64, 1) * s128_ref[...]

        def rope64(x):
            rot = jnp.where(low32, pltpu.roll(x, 96, 1), pltpu.roll(x, 32, 1))
            return x * c64_ref[...] + rot * s64_ref[...]
    else:
        rope128 = rope64 = lambda x: x

    first_half = lax.broadcasted_iota(jnp.int32, (tp, LANES), 1) < 64

    for h in range(GQA_HEADS):
        x = rope128(_rms(cols(C_GQ_Q + LANES * h), gq_ref[...]))
        a_ref[A_GQ_Q + h] = (x * QS_HD).astype(BF16)
    for g in range(GQA_KV_HEADS):
        x = rope128(_rms(cols(C_GQ_K + LANES * g), gk_ref[...]))
        a_ref[A_GQ_K + g] = x.astype(BF16)
        vt_ref[VT_GQ + g] = cols(C_GQ_V + LANES * g).T.astype(BF16)

    for h in range(DIFF_HEADS):
        x = rope64(cols(C_DF_Q + LANES * h)) * QS_DIFF
        a_ref[A_DF_Q + 2 * h] = jnp.where(first_half, x, 0.0).astype(BF16)
        a_ref[A_DF_Q + 2 * h + 1] = jnp.where(first_half, 0.0, x).astype(BF16)
        a_ref[A_DF_K + h] = rope64(cols(C_DF_K + LANES * h)).astype(BF16)
        vt_ref[VT_DF + h] = cols(C_DF_V + LANES * h).T.astype(BF16)

    for h in range(WIN_HEADS):
        a_ref[A_WN_Q + h] = (rope128(cols(C_WN_Q + LANES * h)) * QS_HD).astype(BF16)
    for g in range(WIN_KV_HEADS):
        a_ref[A_WN_K + g] = rope128(cols(C_WN_K + LANES * g)).astype(BF16)
        a_ref[A_WN_V + g] = p_ref[:, C_WN_V + LANES * g:C_WN_V + LANES * (g + 1)]

    hq = _rms(cols(C_ML_CQ, MLA_Q_RANK), qn_ref[...]).astype(BF16)
    qa = _dot(hq, wuq_ref[...])
    hkv = _rms(cols(C_ML_CKV, MLA_KV_RANK), kvn_ref[...]).astype(BF16)
    kva = _dot(hkv, wukv_ref[...])
    kr = rope64(cols(C_ML_KR)).astype(BF16)
    for h in range(MLA_HEADS):
        lo = MLA_QK_PAD * h
        m_ref[M_Q + h, :, 0:LANES] = (qa[:, lo:lo + LANES] * QS_MLA).astype(BF16)
        m_ref[M_Q + h, :, LANES:2 * LANES] = (
            rope64(qa[:, lo + LANES:lo + 2 * LANES]) * QS_MLA).astype(BF16)
        m_ref[M_K + h, :, 0:LANES] = kva[:, LANES * h:LANES * (h + 1)].astype(BF16)
        m_ref[M_K + h, :, LANES:2 * LANES] = kr
        v_lo = MLA_HEADS * MLA_NOPE + MLA_V * h
        vt_ref[VT_ML + h] = kva[:, v_lo:v_lo + MLA_V].T.astype(BF16)


def _prep(p, tables, lp, tp, tiles_per_seq):
    n = p.shape[0]
    rotate = tables is not None
    const = lambda i: (0, 0)
    in_specs = [pl.BlockSpec((tp, P_COLS), lambda i: (i, 0))]
    args = [p]
    if rotate:
        tab = pl.BlockSpec((tp, LANES), lambda i: (i % tiles_per_seq, 0))
        in_specs += [tab] * 4
        args += list(tables)
    in_specs += [
        pl.BlockSpec((1, HEAD_DIM), const),
        pl.BlockSpec((1, HEAD_DIM), const),
        pl.BlockSpec((1, MLA_Q_RANK), const),
        pl.BlockSpec((MLA_Q_RANK, MLA_HEADS * MLA_QK_PAD), const),
        pl.BlockSpec((1, MLA_KV_RANK), const),
        pl.BlockSpec((MLA_KV_RANK, MLA_HEADS * (MLA_NOPE + MLA_V)), const),
    ]
    args += [lp['gqa_q_norm'], lp['gqa_k_norm'], lp['mla_q_norm'], lp['w_uq'],
             lp['mla_kv_norm'], lp['w_ukv']]
    return pl.pallas_call(
        functools.partial(_prep_kernel, rotate=rotate),
        grid=(n // tp,),
        in_specs=in_specs,
        out_specs=[
            pl.BlockSpec((A_SLOTS, tp, LANES), lambda i: (0, i, 0)),
            pl.BlockSpec((M_SLOTS, tp, MLA_QK_PAD), lambda i: (0, i, 0)),
            pl.BlockSpec((VT_SLOTS, None, LANES, tp), lambda i: (0, i, 0, 0)),
        ],
        out_shape=[
            jax.ShapeDtypeStruct((A_SLOTS, n, LANES), BF16),
            jax.ShapeDtypeStruct((M_SLOTS, n, MLA_QK_PAD), BF16),
            jax.ShapeDtypeStruct((VT_SLOTS, n // tp, LANES, tp), BF16),
        ],
        compiler_params=_cparams(("parallel",)),
    )(*args)


def _flash_kernel(*refs, mode, has_latent, tk, cpi, lam_init):
    q_ref, kc_ref, vct_ref = refs[:3]
    refs = refs[3:]
    if has_latent:
        kl_ref, vlt_ref = refs[:2]
        refs = refs[2:]
    if mode == 'diff':
        lq1_ref, lk1_ref, lq2_ref, lk2_ref, sub_ref = refs[:5]
        refs = refs[5:]
    o_ref, m_scr, l_scr, acc_scr = refs
    n_rep, tq, dk = q_ref.shape
    dv = vct_ref.shape[-2]
    sub_per_rep = tq // Q_SUB
    n_sub = n_rep * sub_per_rep

    def q_sub(j):
        r, s = divmod(j, sub_per_rep)
        return q_ref[r, Q_SUB * s:Q_SUB * (s + 1), :]

    def pv(vt_ref, first_chunk, n_chunks, p):
        out = None
        width = vt_ref.shape[-1]
        for c in range(n_chunks):
            part = _dot(vt_ref[first_chunk + c], p[width * c:width * (c + 1)])
            out = part if out is None else out + part
        return out

    def pipeline(units):
        def scores(unit):
            return _dot_nt(unit[0](), q_sub(unit[-1]))

        pending = [scores(u) for u in units[:QK_LOOKAHEAD]]
        for idx, (_, vt_ref, first_chunk, n_chunks, first, j) in enumerate(units):
            st = pending.pop(0)
            if idx + QK_LOOKAHEAD < len(units):
                pending.append(scores(units[idx + QK_LOOKAHEAD]))
            m_new = jnp.max(st, axis=0, keepdims=True)
            if not first:
                m_old = m_scr[j]
                m_new = jnp.maximum(m_old, m_new)
                alpha = jnp.exp2(m_old - m_new)
            p = jnp.exp2(st - m_new)
            l_new = jnp.sum(p, axis=0, keepdims=True)
            acc_new = pv(vt_ref, first_chunk, n_chunks, p.astype(BF16))
            m_scr[j] = m_new
            l_scr[j] = l_new if first else alpha * l_scr[j] + l_new
            acc_scr[j] = acc_new if first else alpha * acc_scr[j] + acc_new

    ctx_units = [(lambda: kc_ref[...], vct_ref, 0, kc_ref.shape[0] // vct_ref.shape[-1], True, j)
                 for j in range(n_sub)]
    if not has_latent:
        pipeline(ctx_units)
    else:
        vt_per_chunk = tk // vlt_ref.shape[-1]
        n_iter = kl_ref.shape[0] // (tk * cpi)

        def latent_units(it):
            units = []
            for o in range(cpi):
                chunk = it * cpi + o
                if isinstance(chunk, int):
                    load_k = lambda lo=chunk * tk: kl_ref[lo:lo + tk, :]
                else:
                    load_k = lambda lo=chunk * tk: kl_ref[pl.ds(pl.multiple_of(lo, tk), tk), :]
                units += [(load_k, vlt_ref, chunk * vt_per_chunk, vt_per_chunk, False, j)
                          for j in range(n_sub)]
            return units

        if n_iter == 1:
            pipeline(ctx_units + latent_units(0))
        else:
            pipeline(ctx_units)

            def body(it, carry):
                pipeline(latent_units(it))
                return carry

            lax.fori_loop(0, n_iter, body, 0)

    def out_t(j):
        return acc_scr[j] / l_scr[j]

    if mode == 'plain':
        for j in range(n_sub):
            r, s = divmod(j, sub_per_rep)
            o_ref[Q_SUB * s:Q_SUB * (s + 1), dv * r:dv * (r + 1)] = out_t(j).T.astype(o_ref.dtype)
    else:
        lam = (jnp.exp(jnp.sum(lq1_ref[...] * lk1_ref[...], keepdims=True))
               - jnp.exp(jnp.sum(lq2_ref[...] * lk2_ref[...], keepdims=True)) + lam_init)
        for s in range(sub_per_rep):
            d = (out_t(s) - lam * out_t(sub_per_rep + s)).T
            o_ref[Q_SUB * s:Q_SUB * (s + 1), :] = (
                _rms(d, sub_ref[...]) * (1.0 - lam_init)).astype(o_ref.dtype)


def _flash(q_arr, q_slot, n_rep, k_lat, k_ctx, k_slot, vt_lat, vt_ctx, v_slot, n_kv, out_w,
           batch, q_is_ctx, tq, mode='plain', diff_params=None, lam_init=0.0):
    dk = q_arr.shape[-1]
    dv = vt_ctx.shape[-2]
    n_q = q_arr.shape[1]
    t_ctx = k_ctx.shape[1] // batch
    nq = n_q // batch // tq
    has_latent = not q_is_ctx
    q_blk = q_slot // n_rep
    w_ctx = vt_ctx.shape[-1]
    assert tq % Q_SUB == 0 and t_ctx % w_ctx == 0
    in_specs = [
        pl.BlockSpec((n_rep, tq, dk), lambda b, g, i: (q_blk + g, b * nq + i, 0)),
        pl.BlockSpec((None, t_ctx, dk), lambda b, g, i: (k_slot + g, b, 0)),
        pl.BlockSpec((None, t_ctx // w_ctx, dv, w_ctx), lambda b, g, i: (v_slot + g, b, 0, 0)),
    ]
    args = [q_arr, k_ctx, vt_ctx]
    tk = cpi = 0
    if has_latent:
        t_lat = k_lat.shape[1] // batch
        w_lat = vt_lat.shape[-1]
        tk = min(512, t_lat)
        cpi = min(FLASH_CHUNKS_PER_ITER, t_lat // tk)
        assert t_lat % (tk * cpi) == 0 and tk % w_lat == 0
        in_specs += [
            pl.BlockSpec((None, t_lat, dk), lambda b, g, i: (k_slot + g, b, 0)),
            pl.BlockSpec((None, t_lat // w_lat, dv, w_lat), lambda b, g, i: (v_slot + g, b, 0, 0)),
        ]
        args += [k_lat, vt_lat]
    if mode == 'diff':
        in_specs += [pl.BlockSpec((1, DIFF_QK_DIM), lambda b, g, i: (0, 0))] * 4
        in_specs += [pl.BlockSpec((1, dv), lambda b, g, i: (0, 0))]
        args += list(diff_params)
    n_sub = n_rep * tq // Q_SUB
    return pl.pallas_call(
        functools.partial(_flash_kernel, mode=mode, has_latent=has_latent, tk=tk, cpi=cpi,
                          lam_init=lam_init),
        grid=(batch, n_kv, nq),
        in_specs=in_specs,
        out_specs=pl.BlockSpec((tq, out_w), lambda b, g, i: (b * nq + i, g)),
        out_shape=jax.ShapeDtypeStruct((n_q, n_kv * out_w), BF16),
        scratch_shapes=[pltpu.VMEM((n_sub, 1, Q_SUB), F32), pltpu.VMEM((n_sub, 1, Q_SUB), F32),
                        pltpu.VMEM((n_sub, dv, Q_SUB), F32)],
        compiler_params=_cparams(("parallel", "parallel", "arbitrary")),
    )(*args)


def _win_kernel(*refs, has_band, band_w, t_lat):
    sink_ref, q_ref, kc_ref, vc_ref = refs[:4]
    refs = refs[4:]
    if has_band:
        kl_ref, vl_ref = refs[:2]
        refs = refs[2:]
    (o_ref,) = refs
    n_rep, tq, dk = q_ref.shape
    dv = vc_ref.shape[-1]
    sub_per_rep = tq // Q_SUB
    n_sub = n_rep * sub_per_rep
    g = pl.program_id(1)
    i = pl.program_id(2)

    kc = kc_ref[...]
    vct = vc_ref[...].astype(F32).T.astype(BF16)
    if has_band:
        start = pl.multiple_of(jnp.clip(i * tq - WINDOW, 0, t_lat - band_w), LANES)
        kb = kl_ref[pl.ds(start, band_w), :]
        vbt = vl_ref[pl.ds(start, band_w), :].astype(F32).T.astype(BF16)
        k_pos = start + lax.broadcasted_iota(jnp.int32, (band_w, Q_SUB), 0)
        q_pos = i * tq + lax.broadcasted_iota(jnp.int32, (band_w, Q_SUB), 1)

    def scores(j):
        r, s = divmod(j, sub_per_rep)
        q = q_ref[r, Q_SUB * s:Q_SUB * (s + 1), :]
        return _dot_nt(kc, q), (_dot_nt(kb, q) if has_band else None)

    pending = [scores(j) for j in range(n_sub)]
    for j in range(n_sub):
        r, s = divmod(j, sub_per_rep)
        st_c, st_b = pending[j]
        sink = sink_ref[g * n_rep + r] * LOG2E
        m = jnp.maximum(jnp.max(st_c, axis=0, keepdims=True), sink)
        if has_band:
            in_window = jnp.abs(q_pos + Q_SUB * s - k_pos) <= WINDOW
            st_b = jnp.where(in_window, st_b, NEG_INF)
            m = jnp.maximum(m, jnp.max(st_b, axis=0, keepdims=True))
        p_c = jnp.exp2(st_c - m)
        l = jnp.sum(p_c, axis=0, keepdims=True) + jnp.exp2(sink - m)
        acc = _dot(vct, p_c.astype(BF16))
        if has_band:
            p_b = jnp.exp2(st_b - m)
            l = l + jnp.sum(p_b, axis=0, keepdims=True)
            acc = acc + _dot(vbt, p_b.astype(BF16))
        o_ref[Q_SUB * s:Q_SUB * (s + 1), dv * r:dv * (r + 1)] = (acc / l).T.astype(o_ref.dtype)


def _window(sinks, a_q, a_lat, a_ctx, batch, q_is_ctx, tq):
    n_rep = WIN_HEADS // WIN_KV_HEADS
    n_q = a_q.shape[1]
    t_ctx = a_ctx.shape[1] // batch
    nq = n_q // batch // tq
    has_band = not q_is_ctx
    q_blk = A_WN_Q // n_rep
    in_specs = [
        pl.BlockSpec(memory_space=pltpu.SMEM),
        pl.BlockSpec((n_rep, tq, HEAD_DIM), lambda b, g, i: (q_blk + g, b * nq + i, 0)),
        pl.BlockSpec((None, t_ctx, HEAD_DIM), lambda b, g, i: (A_WN_K + g, b, 0)),
        pl.BlockSpec((None, t_ctx, HEAD_DIM), lambda b, g, i: (A_WN_V + g, b, 0)),
    ]
    args = [sinks, a_q, a_ctx, a_ctx]
    band_w = t_lat = 0
    if has_band:
        t_lat = a_lat.shape[1] // batch
        band_w = min(tq + 2 * WINDOW, t_lat)
        assert tq & (tq - 1) == 0
        in_specs += [
            pl.BlockSpec((None, t_lat, HEAD_DIM), lambda b, g, i: (A_WN_K + g, b, 0)),
            pl.BlockSpec((None, t_lat, HEAD_DIM), lambda b, g, i: (A_WN_V + g, b, 0)),
        ]
        args += [a_lat, a_lat]
    return pl.pallas_call(
        functools.partial(_win_kernel, has_band=has_band, band_w=band_w, t_lat=t_lat),
        grid=(batch, WIN_KV_HEADS, nq),
        in_specs=in_specs,
        out_specs=pl.BlockSpec((tq, n_rep * HEAD_DIM), lambda b, g, i: (b * nq + i, g)),
        out_shape=jax.ShapeDtypeStruct((n_q, GROUP_WIDTH), BF16),
        compiler_params=_cparams(("parallel", "parallel", "arbitrary")),
    )(*args)


def _outproj_kernel(a0_ref, a1_ref, a2_ref, a3_ref, w_ref, x_ref, mod_ref, gpost_ref, gpre_ref,
                    xo_ref, h_ref):
    piece = x_ref.shape[0] // OUTPROJ_PIECES

    def project(p):
        rows = slice(piece * p, piece * (p + 1))
        y = None
        for k, a_ref in enumerate((a0_ref, a1_ref, a2_ref, a3_ref)):
            part = _dot(a_ref[rows, :], w_ref[GROUP_WIDTH * k:GROUP_WIDTH * (k + 1), :])
            y = part if y is None else y + part
        return y

    pending = project(0)
    for p in range(OUTPROJ_PIECES):
        rows = slice(piece * p, piece * (p + 1))
        y = pending
        if p + 1 < OUTPROJ_PIECES:
            pending = project(p + 1)
        xm = x_ref[rows, :] + mod_ref[2:3, :] * _rms(y, gpost_ref[...])
        xo_ref[rows, :] = xm
        h_ref[rows, :] = (
            _rms(xm, gpre_ref[...]) * (1.0 + mod_ref[4:5, :]) + mod_ref[3:4, :]).astype(BF16)


def _outproj(mix, w_out, x, mod, g_post, g_pre, tm, mod_idx):
    n, d = x.shape
    const = lambda i: (0, 0)
    return pl.pallas_call(
        _outproj_kernel,
        grid=(n // tm,),
        in_specs=[pl.BlockSpec((tm, GROUP_WIDTH), lambda i: (i, 0))] * 4 + [
            pl.BlockSpec(w_out.shape, const),
            pl.BlockSpec((tm, d), lambda i: (i, 0)),
            pl.BlockSpec((None, 6, d), lambda i: (mod_idx(i), 0, 0)),
            pl.BlockSpec((1, d), const),
            pl.BlockSpec((1, d), const),
        ],
        out_specs=[pl.BlockSpec((tm, d), lambda i: (i, 0)), pl.BlockSpec((tm, d), lambda i: (i, 0))],
        out_shape=[jax.ShapeDtypeStruct((n, d), F32), jax.ShapeDtypeStruct((n, d), BF16)],
        compiler_params=_cparams(("parallel",)),
    )(*mix, w_out, x, mod, g_post, g_pre)


def _ffn_kernel(h_ref, wg_ref, wu_ref, wd_ref, x_ref, mod_ref, g_ref, o_ref):
    f = pl.program_id(1)
    piece = h_ref.shape[0] // FFN_PIECES

    @pl.when(f == 0)
    def _():
        o_ref[...] = jnp.zeros_like(o_ref)

    def gate_up(p):
        hp = h_ref[piece * p:piece * (p + 1), :]
        return _dot(hp, wg_ref[...]), _dot(hp, wu_ref[...])

    pending = gate_up(0)
    for p in range(FFN_PIECES):
        gate, up = pending
        if p + 1 < FFN_PIECES:
            pending = gate_up(p + 1)
        act = (gate / (1.0 + jnp.exp(-gate)) * up).astype(BF16)
        o_ref[piece * p:piece * (p + 1), :] += _dot(act, wd_ref[...])

    @pl.when(f == pl.num_programs(1) - 1)
    def _():
        o_ref[...] = x_ref[...] + mod_ref[5:6, :] * _rms(o_ref[...], g_ref[...])


def _ffn(h, wg, wu, wd, x, mod, g_post, tm, tf, mod_idx):
    n, d = x.shape
    d_ff = wd.shape[0]
    return pl.pallas_call(
        _ffn_kernel,
        grid=(n // tm, d_ff // tf),
        in_specs=[
            pl.BlockSpec((tm, d), lambda i, f: (i, 0)),
            pl.BlockSpec((d, tf), lambda i, f: (0, f)),
            pl.BlockSpec((d, tf), lambda i, f: (0, f)),
            pl.BlockSpec((tf, d), lambda i, f: (f, 0)),
            pl.BlockSpec((tm, d), lambda i, f: (i, 0)),
            pl.BlockSpec((None, 6, d), lambda i, f: (mod_idx(i), 0, 0)),
            pl.BlockSpec((1, d), lambda i, f: (0, 0)),
        ],
        out_specs=pl.BlockSpec((tm, d), lambda i, f: (i, 0)),
        out_shape=jax.ShapeDtypeStruct((n, d), F32),
        compiler_params=_cparams(("parallel", "arbitrary")),
    )(h, wg, wu, wd, x, mod, g_post)


def _rope_tables(n_tokens, dim):
    rows = n_tokens // GRID_W
    row = jnp.broadcast_to(jnp.arange(rows)[:, None], (rows, GRID_W)).reshape(-1).astype(F32)
    col = jnp.broadcast_to(jnp.arange(GRID_W)[None, :], (rows, GRID_W)).reshape(-1).astype(F32)
    quarter = dim // 4
    inv_freq = ROPE_BASE ** (-jnp.arange(quarter, dtype=F32) / quarter)
    ang = jnp.concatenate([row[:, None] * inv_freq, col[:, None] * inv_freq], axis=-1)
    cos, sin = jnp.cos(ang), jnp.sin(ang)
    reps = LANES // dim
    return (jnp.tile(jnp.concatenate([cos, cos], axis=-1), (1, reps)),
            jnp.tile(jnp.concatenate([-sin, sin], axis=-1), (1, reps)))


def _layer_weights(l, w_in, mla_w_uq, mla_w_ukv, w_out, ffn_w_gate, ffn_w_up, ffn_w_down):
    w_uq = mla_w_uq[l].reshape(MLA_Q_RANK, MLA_HEADS, MLA_QK_DIM)
    w_uq = jnp.pad(w_uq, ((0, 0), (0, 0), (0, MLA_QK_PAD - MLA_QK_DIM)))
    w_ukv = mla_w_ukv[l].reshape(MLA_KV_RANK, MLA_HEADS, 2, MLA_NOPE).transpose(0, 2, 1, 3)
    return {
        'w_in': jnp.pad(w_in[l], ((0, 0), (0, P_COLS - IN_COLS))).astype(BF16),
        'w_uq': w_uq.reshape(MLA_Q_RANK, MLA_HEADS * MLA_QK_PAD).astype(BF16),
        'w_ukv': w_ukv.reshape(MLA_KV_RANK, MLA_HEADS * (MLA_NOPE + MLA_V)).astype(BF16),
        'w_out': w_out[l].astype(BF16),
        'wg': ffn_w_gate[l].astype(BF16),
        'wu': ffn_w_up[l].astype(BF16),
        'wd': ffn_w_down[l].astype(BF16),
    }


def _mixers(lp, prep_q, prep_lat, prep_ctx, batch, q_is_ctx, t_q, lam_init):
    a_q, m_q, _ = prep_q
    a_lat, m_lat, vt_lat = prep_lat if prep_lat is not None else (None, None, None)
    a_ctx, m_ctx, vt_ctx = prep_ctx
    tq = lambda n_rep: min(FLASH_ROWS // n_rep, t_q)
    gq = _flash(a_q, A_GQ_Q, GQA_HEADS // GQA_KV_HEADS, a_lat, a_ctx, A_GQ_K, vt_lat, vt_ctx, VT_GQ,
                GQA_KV_HEADS, 2 * HEAD_DIM, batch, q_is_ctx, tq(2))
    df = _flash(a_q, A_DF_Q, 2, a_lat, a_ctx, A_DF_K, vt_lat, vt_ctx, VT_DF,
                DIFF_HEADS, HEAD_DIM, batch, q_is_ctx, tq(2), mode='diff',
                diff_params=lp['diff'], lam_init=lam_init)
    wn = _window(lp['win_sinks'], a_q, a_lat, a_ctx, batch, q_is_ctx, min(512, t_q))
    ml = _flash(m_q, M_Q, 1, m_lat, m_ctx, M_K, vt_lat, vt_ctx, VT_ML,
                MLA_HEADS, MLA_V, batch, q_is_ctx, tq(1))
    return gq, df, wn, ml


def kernel(x, c, ctx, c_ctx, ada_w, ada_b, norm_pre_mix, norm_post_mix, norm_pre_ffn, norm_post_ffn,
           w_in, gqa_q_norm, gqa_k_norm, diff_lambda_q1, diff_lambda_k1, diff_lambda_q2, diff_lambda_k2,
           diff_subln, win_sinks, mla_q_norm, mla_w_uq, mla_kv_norm, mla_w_ukv, w_out,
           ffn_w_gate, ffn_w_up, ffn_w_down):
    batch, t_lat, d = x.shape
    t_ctx = ctx.shape[1]
    n_layers = ada_w.shape[0]
    d_ff = ffn_w_gate.shape[-1]
    assert w_in.shape[-1] == IN_COLS and t_lat % GRID_W == 0

    tm_lat = min(1024, t_lat)
    tm_ctx = min(1024, batch * t_ctx)
    to_lat = min(512, t_lat)
    to_ctx = min(512, batch * t_ctx)
    tff_lat = min(512, t_lat)
    tff_ctx = min(512, batch * t_ctx)
    tf = 512 if d_ff % 512 == 0 else 256

    mod_rows = 16
    cc = jnp.zeros((mod_rows, d), F32).at[:batch].set(c).at[batch].set(c_ctx)
    mod_all = _modulation(cc, ada_w, ada_b).reshape(n_layers, mod_rows, 6, d)

    tables = _rope_tables(t_lat, HEAD_DIM) + _rope_tables(t_lat, DIFF_QK_DIM)

    x_lat = x.reshape(batch * t_lat, d)
    x_ctx = ctx.reshape(batch * t_ctx, d)
    ctx_mod = lambda i: batch

    row2 = lambda a: a.reshape(1, -1)
    for l in range(n_layers):
        last = l == n_layers - 1
        lam_init = 0.8 - 0.6 * math.exp(-0.3 * l)
        lp = _layer_weights(l, w_in, mla_w_uq, mla_w_ukv, w_out, ffn_w_gate, ffn_w_up, ffn_w_down)
        lp.update({
            'gqa_q_norm': row2(gqa_q_norm[l]), 'gqa_k_norm': row2(gqa_k_norm[l]),
            'mla_q_norm': row2(mla_q_norm[l]), 'mla_kv_norm': row2(mla_kv_norm[l]),
            'diff': (row2(diff_lambda_q1[l]), row2(diff_lambda_k1[l]), row2(diff_lambda_q2[l]),
                     row2(diff_lambda_k2[l]), row2(diff_subln[l])),
            'win_sinks': win_sinks[l],
        })
        mod = mod_all[l]
        g_pre_mix, g_post_mix = row2(norm_pre_mix[l]), row2(norm_post_mix[l])
        g_pre_ffn, g_post_ffn = row2(norm_pre_ffn[l]), row2(norm_post_ffn[l])

        def lat_mod(tile):
            return lambda i: (i * tile) // t_lat

        p_lat = _inproj(x_lat, mod, g_pre_mix, lp['w_in'], tm_lat, lat_mod(tm_lat))
        p_ctx = _inproj(x_ctx, mod, g_pre_mix, lp['w_in'], tm_ctx, ctx_mod)
        tp_lat, tp_ctx = min(KEY_CHUNK, t_lat), min(KEY_CHUNK, t_ctx)
        prep_lat = _prep(p_lat, tables, lp, tp_lat, t_lat // tp_lat)
        prep_ctx = _prep(p_ctx, None, lp, tp_ctx, 1)

        mix = _mixers(lp, prep_lat, prep_lat, prep_ctx, batch, False, t_lat, lam_init)
        x_mid, h2 = _outproj(mix, lp['w_out'], x_lat, mod, g_post_mix, g_pre_ffn, to_lat, lat_mod(to_lat))
        x_lat = _ffn(h2, lp['wg'], lp['wu'], lp['wd'], x_mid, mod, g_post_ffn, tff_lat, tf, lat_mod(tff_lat))

        if not last:
            mix_c = _mixers(lp, prep_ctx, None, prep_ctx, batch, True, t_ctx, lam_init)
            xc_mid, hc2 = _outproj(mix_c, lp['w_out'], x_ctx, mod, g_post_mix, g_pre_ffn, to_ctx, ctx_mod)
            x_ctx = _ffn(hc2, lp['wg'], lp['wu'], lp['wd'], xc_mid, mod, g_post_ffn, tff_ctx, tf, ctx_mod)

    return x_lat.reshape(batch, t_lat, d)
```

```python
import functools
import math

import jax
import jax.numpy as jnp
from jax import lax
from jax.experimental import pallas as pl
from jax.experimental.pallas import tpu as pltpu

F32 = jnp.float32
BF16 = jnp.bfloat16

GRID_W = 64
HEAD_DIM = 128
GQA_HEADS, GQA_KV_HEADS = 4, 2
DIFF_HEADS, DIFF_QK_DIM = 4, 64
WIN_HEADS, WIN_KV_HEADS, WINDOW = 4, 2, 128
MLA_HEADS, MLA_Q_RANK, MLA_KV_RANK = 4, 512, 256
MLA_NOPE, MLA_ROPE, MLA_V = 128, 64, 128
MLA_QK_DIM = MLA_NOPE + MLA_ROPE
MLA_QK_PAD = 256
GROUP_WIDTH = 512
ROPE_BASE = 10000.0
NORM_EPS = 1e-6
NEG_INF = -1e30
LOG2E = math.log2(math.e)
LANES = 128

IN_COLS = 4416
P_COLS = 4608
C_GQ_Q, C_GQ_K, C_GQ_V = 0, 512, 768
C_DF_Q, C_DF_K, C_DF_V = 1024, 1536, 2048
C_WN_Q, C_WN_K, C_WN_V = 2560, 3072, 3328
C_ML_CQ, C_ML_CKV, C_ML_KR = 3584, 4096, 4352

A_GQ_Q, A_GQ_K = 0, 4
A_DF_Q, A_DF_K = 6, 14
A_WN_Q, A_WN_K, A_WN_V = 18, 22, 24
A_SLOTS = 26
VT_GQ, VT_DF, VT_ML = 0, 2, 6
VT_SLOTS = 10
KEY_CHUNK = 512
Q_SUB = 256
FLASH_ROWS = 2048
QK_LOOKAHEAD = 8
INPROJ_PIECES = 4
INPROJ_TN = 1536
OUTPROJ_PIECES = 2
FLASH_TK = 512
FLASH_CHUNKS_PER_ITER = 8
FFN_PIECES = 2
M_Q, M_K = 0, 4
M_SLOTS = 8

QS_HD = HEAD_DIM ** -0.5 * LOG2E
QS_DIFF = DIFF_QK_DIM ** -0.5 * LOG2E
QS_MLA = MLA_QK_DIM ** -0.5 * LOG2E

VMEM_LIMIT = 56 * 1024 * 1024


def _cparams(sem):
    return pltpu.CompilerParams(dimension_semantics=sem, vmem_limit_bytes=VMEM_LIMIT)


def _rms(x, g):
    return x * lax.rsqrt(jnp.mean(x * x, axis=-1, keepdims=True) + NORM_EPS) * g


def _dot(a, b):
    return jnp.dot(a, b, preferred_element_type=F32)


def _dot_nt(a, b):
    return lax.dot_general(a, b, (((1,), (1,)), ((), ())), preferred_element_type=F32)


def _mod_kernel(c_ref, w_ref, b_ref, o_ref):
    c = c_ref[...]
    s = (c / (1.0 + jnp.exp(-c))).astype(BF16)
    o_ref[...] = _dot(s, w_ref[...].astype(BF16)) + b_ref[...]


def _modulation(cc, ada_w, ada_b):
    n_layers, d, d6 = ada_w.shape
    rows = cc.shape[0]
    tn = d // 2
    return pl.pallas_call(
        _mod_kernel,
        grid=(n_layers, d6 // tn),
        in_specs=[
            pl.BlockSpec((rows, d), lambda l, j: (0, 0)),
            pl.BlockSpec((None, d, tn), lambda l, j: (l, 0, j)),
            pl.BlockSpec((None, 1, tn), lambda l, j: (l, 0, j)),
        ],
        out_specs=pl.BlockSpec((None, rows, tn), lambda l, j: (l, 0, j)),
        out_shape=jax.ShapeDtypeStruct((n_layers, rows, d6), F32),
        compiler_params=_cparams(("parallel", "parallel")),
    )(cc, ada_w, ada_b.reshape(n_layers, 1, d6))


def _inproj_kernel(x_ref, mod_ref, g_ref, w_ref, o_ref, h_scr):
    j = pl.program_id(1)
    piece = x_ref.shape[0] // INPROJ_PIECES

    @pl.when(j == 0)
    def _():
        for p in range(INPROJ_PIECES):
            rows = slice(piece * p, piece * (p + 1))
            h = _rms(x_ref[rows, :], g_ref[...])
            h = (h * (1.0 + mod_ref[1:2, :]) + mod_ref[0:1, :]).astype(BF16)
            h_scr[rows, :] = h
            o_ref[rows, :] = _dot(h, w_ref[...]).astype(BF16)

    @pl.when(j > 0)
    def _():
        o_ref[...] = _dot(h_scr[...], w_ref[...]).astype(BF16)


def _inproj(x, mod, gain, w, tm, mod_idx):
    n, d = x.shape
    tn = INPROJ_TN
    return pl.pallas_call(
        _inproj_kernel,
        grid=(n // tm, P_COLS // tn),
        in_specs=[
            pl.BlockSpec((tm, d), lambda i, j: (i, 0)),
            pl.BlockSpec((None, 6, d), lambda i, j: (mod_idx(i), 0, 0)),
            pl.BlockSpec((1, d), lambda i, j: (0, 0)),
            pl.BlockSpec((d, tn), lambda i, j: (0, j)),
        ],
        out_specs=pl.BlockSpec((tm, tn), lambda i, j: (i, j)),
        out_shape=jax.ShapeDtypeStruct((n, P_COLS), BF16),
        scratch_shapes=[pltpu.VMEM((tm, d), BF16)],
        compiler_params=_cparams(("parallel", "arbitrary")),
    )(x, mod, gain, w)


def _prep_kernel(*refs, rotate):
    if rotate:
        p_ref, c128_ref, s128_ref, c64_ref, s64_ref = refs[:5]
        refs = refs[5:]
    else:
        p_ref = refs[0]
        refs = refs[1:]
    gq_ref, gk_ref, qn_ref, wuq_ref, kvn_ref, wukv_ref, a_ref, m_ref, vt_ref = refs
    tp = p_ref.shape[0]

    def cols(lo, width=LANES):
        return p_ref[:, lo:lo + width].astype(F32)

    if rotate:
        lane = lax.broadcasted_iota(jnp.int32, (tp, LANES), 1)
        low32 = (lane & 63) < 32

        def rope128(x):
            return x * c128_ref[...] + pltpu.roll(x, 64, 1) * s128_ref[...]

        def rope64(x):
            rot = jnp.where(low32, pltpu.roll(x, 96, 1), pltpu.roll(x, 32, 1))
            return x * c64_ref[...] + rot * s64_ref[...]
    else:
        rope128 = rope64 = lambda x: x

    first_half = lax.broadcasted_iota(jnp.int32, (tp, LANES), 1) < 64

    for h in range(GQA_HEADS):
        x = rope128(_rms(cols(C_GQ_Q + LANES * h), gq_ref[...]))
        a_ref[A_GQ_Q + h] = (x * QS_HD).astype(BF16)
    for g in range(GQA_KV_HEADS):
        x = rope128(_rms(cols(C_GQ_K + LANES * g), gk_ref[...]))
        a_ref[A_GQ_K + g] = x.astype(BF16)
        vt_ref[VT_GQ + g] = cols(C_GQ_V + LANES * g).T.astype(BF16)

    for h in range(DIFF_HEADS):
        x = rope64(cols(C_DF_Q + LANES * h)) * QS_DIFF
        a_ref[A_DF_Q + 2 * h] = jnp.where(first_half, x, 0.0).astype(BF16)
        a_ref[A_DF_Q + 2 * h + 1] = jnp.where(first_half, 0.0, x).astype(BF16)
        a_ref[A_DF_K + h] = rope64(cols(C_DF_K + LANES * h)).astype(BF16)
        vt_ref[VT_DF + h] = cols(C_DF_V + LANES * h).T.astype(BF16)

    for h in range(WIN_HEADS):
        a_ref[A_WN_Q + h] = (rope128(cols(C_WN_Q + LANES * h)) * QS_HD).astype(BF16)
    for g in range(WIN_KV_HEADS):
        a_ref[A_WN_K + g] = rope128(cols(C_WN_K + LANES * g)).astype(BF16)
        a_ref[A_WN_V + g] = p_ref[:, C_WN_V + LANES * g:C_WN_V + LANES * (g + 1)]

    hq = _rms(cols(C_ML_CQ, MLA_Q_RANK), qn_ref[...]).astype(BF16)
    qa = _dot(hq, wuq_ref[...])
    hkv = _rms(cols(C_ML_CKV, MLA_KV_RANK), kvn_ref[...]).astype(BF16)
    kva = _dot(hkv, wukv_ref[...])
    kr = rope64(cols(C_ML_KR)).astype(BF16)
    for h in range(MLA_HEADS):
        lo = MLA_QK_PAD * h
        m_ref[M_Q + h, :, 0:LANES] = (qa[:, lo:lo + LANES] * QS_MLA).astype(BF16)
        m_ref[M_Q + h, :, LANES:2 * LANES] = (
            rope64(qa[:, lo + LANES:lo + 2 * LANES]) * QS_MLA).astype(BF16)
        m_ref[M_K + h, :, 0:LANES] = kva[:, LANES * h:LANES * (h + 1)].astype(BF16)
        m_ref[M_K + h, :, LANES:2 * LANES] = kr
        v_lo = MLA_HEADS * MLA_NOPE + MLA_V * h
        vt_ref[VT_ML + h] = kva[:, v_lo:v_lo + MLA_V].T.astype(BF16)


def _prep(p, tables, lp, tp, tiles_per_seq):
    n = p.shape[0]
    rotate = tables is not None
    const = lambda i: (0, 0)
    in_specs = [pl.BlockSpec((tp, P_COLS), lambda i: (i, 0))]
    args = [p]
    if rotate:
        tab = pl.BlockSpec((tp, LANES), lambda i: (i % tiles_per_seq, 0))
        in_specs += [tab] * 4
        args += list(tables)
    in_specs += [
        pl.BlockSpec((1, HEAD_DIM), const),
        pl.BlockSpec((1, HEAD_DIM), const),
        pl.BlockSpec((1, MLA_Q_RANK), const),
        pl.BlockSpec((MLA_Q_RANK, MLA_HEADS * MLA_QK_PAD), const),
        pl.BlockSpec((1, MLA_KV_RANK), const),
        pl.BlockSpec((MLA_KV_RANK, MLA_HEADS * (MLA_NOPE + MLA_V)), const),
    ]
    args += [lp['gqa_q_norm'], lp['gqa_k_norm'], lp['mla_q_norm'], lp['w_uq'],
             lp['mla_kv_norm'], lp['w_ukv']]
    return pl.pallas_call(
        functools.partial(_prep_kernel, rotate=rotate),
        grid=(n // tp,),
        in_specs=in_specs,
        out_specs=[
            pl.BlockSpec((A_SLOTS, tp, LANES), lambda i: (0, i, 0)),
            pl.BlockSpec((M_SLOTS, tp, MLA_QK_PAD), lambda i: (0, i, 0)),
            pl.BlockSpec((VT_SLOTS, None, LANES, tp), lambda i: (0, i, 0, 0)),
        ],
        out_shape=[
            jax.ShapeDtypeStruct((A_SLOTS, n, LANES), BF16),
            jax.ShapeDtypeStruct((M_SLOTS, n, MLA_QK_PAD), BF16),
            jax.ShapeDtypeStruct((VT_SLOTS, n // tp, LANES, tp), BF16),
        ],
        compiler_params=_cparams(("parallel",)),
    )(*args)


def _flash_kernel(*refs, mode, has_latent, tk, cpi, lam_init):
    q_ref, kc_ref, vct_ref = refs[:3]
    refs = refs[3:]
    if has_latent:
        kl_ref, vlt_ref = refs[:2]
        refs = refs[2:]
    if mode == 'diff':
        lq1_ref, lk1_ref, lq2_ref, lk2_ref, sub_ref = refs[:5]
        refs = refs[5:]
    o_ref, m_scr, l_scr, acc_scr = refs
    n_rep, tq, dk = q_ref.shape
    dv = vct_ref.shape[-2]
    sub_per_rep = tq // Q_SUB
    n_sub = n_rep * sub_per_rep

    def q_sub(j):
        r, s = divmod(j, sub_per_rep)
        return q_ref[r, Q_SUB * s:Q_SUB * (s + 1), :]

    def pv(vt_ref, first_chunk, n_chunks, p):
        out = None
        width = vt_ref.shape[-1]
        for c in range(n_chunks):
            part = _dot(vt_ref[first_chunk + c], p[width * c:width * (c + 1)])
            out = part if out is None else out + part
        return out

    def pipeline(units):
        def scores(unit):
            return _dot_nt(unit[0](), q_sub(unit[-1]))

        pending = [scores(u) for u in units[:QK_LOOKAHEAD]]
        for idx, (_, vt_ref, first_chunk, n_chunks, first, j) in enumerate(units):
            st = pending.pop(0)
            if idx + QK_LOOKAHEAD < len(units):
                pending.append(scores(units[idx + QK_LOOKAHEAD]))
            m_new = jnp.max(st, axis=0, keepdims=True)
            if not first:
                m_old = m_scr[j]
                m_new = jnp.maximum(m_old, m_new)
                alpha = jnp.exp2(m_old - m_new)
            p = jnp.exp2(st - m_new)
            l_new = jnp.sum(p, axis=0, keepdims=True)
            acc_new = pv(vt_ref, first_chunk, n_chunks, p.astype(BF16))
            m_scr[j] = m_new
            l_scr[j] = l_new if first else alpha * l_scr[j] + l_new
            acc_scr[j] = acc_new if first else alpha * acc_scr[j] + acc_new

    ctx_units = [(lambda: kc_ref[...], vct_ref, 0, kc_ref.shape[0] // vct_ref.shape[-1], True, j)
                 for j in range(n_sub)]
    if not has_latent:
        pipeline(ctx_units)
    else:
        vt_per_chunk = tk // vlt_ref.shape[-1]
        n_iter = kl_ref.shape[0] // (tk * cpi)

        def latent_units(it):
            units = []
            for o in range(cpi):
                chunk = it * cpi + o
                if isinstance(chunk, int):
                    load_k = lambda lo=chunk * tk: kl_ref[lo:lo + tk, :]
                else:
                    load_k = lambda lo=chunk * tk: kl_ref[pl.ds(pl.multiple_of(lo, tk), tk), :]
                units += [(load_k, vlt_ref, chunk * vt_per_chunk, vt_per_chunk, False, j)
                          for j in range(n_sub)]
            return units

        if n_iter == 1:
            pipeline(ctx_units + latent_units(0))
        else:
            pipeline(ctx_units)

            def body(it, carry):
                pipeline(latent_units(it))
                return carry

            lax.fori_loop(0, n_iter, body, 0)

    def out_t(j):
        return acc_scr[j] / l_scr[j]

    if mode == 'plain':
        for j in range(n_sub):
            r, s = divmod(j, sub_per_rep)
            o_ref[Q_SUB * s:Q_SUB * (s + 1), dv * r:dv * (r + 1)] = out_t(j).T.astype(o_ref.dtype)
    else:
        lam = (jnp.exp(jnp.sum(lq1_ref[...] * lk1_ref[...], keepdims=True))
               - jnp.exp(jnp.sum(lq2_ref[...] * lk2_ref[...], keepdims=True)) + lam_init)
        for s in range(sub_per_rep):
            d = (out_t(s) - lam * out_t(sub_per_rep + s)).T
            o_ref[Q_SUB * s:Q_SUB * (s + 1), :] = (
                _rms(d, sub_ref[...]) * (1.0 - lam_init)).astype(o_ref.dtype)


def _flash(q_arr, q_slot, n_rep, k_lat, k_ctx, k_slot, vt_lat, vt_ctx, v_slot, n_kv, out_w,
           batch, q_is_ctx, tq, mode='plain', diff_params=None, lam_init=0.0):
    dk = q_arr.shape[-1]
    dv = vt_ctx.shape[-2]
    n_q = q_arr.shape[1]
    t_ctx = k_ctx.shape[1] // batch
    nq = n_q // batch // tq
    has_latent = not q_is_ctx
    q_blk = q_slot // n_rep
    w_ctx = vt_ctx.shape[-1]
    assert tq % Q_SUB == 0 and t_ctx % w_ctx == 0
    in_specs = [
        pl.BlockSpec((n_rep, tq, dk), lambda b, g, i: (q_blk + g, b * nq + i, 0)),
        pl.BlockSpec((None, t_ctx, dk), lambda b, g, i: (k_slot + g, b, 0)),
        pl.BlockSpec((None, t_ctx // w_ctx, dv, w_ctx), lambda b, g, i: (v_slot + g, b, 0, 0)),
    ]
    args = [q_arr, k_ctx, vt_ctx]
    tk = cpi = 0
    if has_latent:
        t_lat = k_lat.shape[1] // batch
        w_lat = vt_lat.shape[-1]
        tk = min(FLASH_TK, t_lat)
        cpi = min(FLASH_CHUNKS_PER_ITER, t_lat // tk)
        assert t_lat % (tk * cpi) == 0 and tk % w_lat == 0
        in_specs += [
            pl.BlockSpec((None, t_lat, dk), lambda b, g, i: (k_slot + g, b, 0)),
            pl.BlockSpec((None, t_lat // w_lat, dv, w_lat), lambda b, g, i: (v_slot + g, b, 0, 0)),
        ]
        args += [k_lat, vt_lat]
    if mode == 'diff':
        in_specs += [pl.BlockSpec((1, DIFF_QK_DIM), lambda b, g, i: (0, 0))] * 4
        in_specs += [pl.BlockSpec((1, dv), lambda b, g, i: (0, 0))]
        args += list(diff_params)
    n_sub = n_rep * tq // Q_SUB
    return pl.pallas_call(
        functools.partial(_flash_kernel, mode=mode, has_latent=has_latent, tk=tk, cpi=cpi,
                          lam_init=lam_init),
        grid=(batch, n_kv, nq),
        in_specs=in_specs,
        out_specs=pl.BlockSpec((tq, out_w), lambda b, g, i: (b * nq + i, g)),
        out_shape=jax.ShapeDtypeStruct((n_q, n_kv * out_w), BF16),
        scratch_shapes=[pltpu.VMEM((n_sub, 1, Q_SUB), F32), pltpu.VMEM((n_sub, 1, Q_SUB), F32),
                        pltpu.VMEM((n_sub, dv, Q_SUB), F32)],
        compiler_params=_cparams(("parallel", "parallel", "arbitrary")),
    )(*args)


def _win_kernel(*refs, has_band, band_w, t_lat):
    sink_ref, q_ref, kc_ref, vc_ref = refs[:4]
    refs = refs[4:]
    if has_band:
        kl_ref, vl_ref = refs[:2]
        refs = refs[2:]
    (o_ref,) = refs
    n_rep, tq, dk = q_ref.shape
    dv = vc_ref.shape[-1]
    sub_per_rep = tq // Q_SUB
    n_sub = n_rep * sub_per_rep
    g = pl.program_id(1)
    i = pl.program_id(2)

    kc = kc_ref[...]
    vct = vc_ref[...].astype(F32).T.astype(BF16)
    if has_band:
        start = pl.multiple_of(jnp.clip(i * tq - WINDOW, 0, t_lat - band_w), LANES)
        kb = kl_ref[pl.ds(start, band_w), :]
        vbt = vl_ref[pl.ds(start, band_w), :].astype(F32).T.astype(BF16)
        k_pos = start + lax.broadcasted_iota(jnp.int32, (band_w, Q_SUB), 0)
        q_pos = i * tq + lax.broadcasted_iota(jnp.int32, (band_w, Q_SUB), 1)

    def scores(j):
        r, s = divmod(j, sub_per_rep)
        q = q_ref[r, Q_SUB * s:Q_SUB * (s + 1), :]
        return _dot_nt(kc, q), (_dot_nt(kb, q) if has_band else None)

    pending = [scores(j) for j in range(n_sub)]
    for j in range(n_sub):
        r, s = divmod(j, sub_per_rep)
        st_c, st_b = pending[j]
        sink = sink_ref[g * n_rep + r] * LOG2E
        m = jnp.maximum(jnp.max(st_c, axis=0, keepdims=True), sink)
        if has_band:
            in_window = jnp.abs(q_pos + Q_SUB * s - k_pos) <= WINDOW
            st_b = jnp.where(in_window, st_b, NEG_INF)
            m = jnp.maximum(m, jnp.max(st_b, axis=0, keepdims=True))
        p_c = jnp.exp2(st_c - m)
        l = jnp.sum(p_c, axis=0, keepdims=True) + jnp.exp2(sink - m)
        acc = _dot(vct, p_c.astype(BF16))
        if has_band:
            p_b = jnp.exp2(st_b - m)
            l = l + jnp.sum(p_b, axis=0, keepdims=True)
            acc = acc + _dot(vbt, p_b.astype(BF16))
        o_ref[Q_SUB * s:Q_SUB * (s + 1), dv * r:dv * (r + 1)] = (acc / l).T.astype(o_ref.dtype)


def _window(sinks, a_q, a_lat, a_ctx, batch, q_is_ctx, tq):
    n_rep = WIN_HEADS // WIN_KV_HEADS
    n_q = a_q.shape[1]
    t_ctx = a_ctx.shape[1] // batch
    nq = n_q // batch // tq
    has_band = not q_is_ctx
    q_blk = A_WN_Q // n_rep
    in_specs = [
        pl.BlockSpec(memory_space=pltpu.SMEM),
        pl.BlockSpec((n_rep, tq, HEAD_DIM), lambda b, g, i: (q_blk + g, b * nq + i, 0)),
        pl.BlockSpec((None, t_ctx, HEAD_DIM), lambda b, g, i: (A_WN_K + g, b, 0)),
        pl.BlockSpec((None, t_ctx, HEAD_DIM), lambda b, g, i: (A_WN_V + g, b, 0)),
    ]
    args = [sinks, a_q, a_ctx, a_ctx]
    band_w = t_lat = 0
    if has_band:
        t_lat = a_lat.shape[1] // batch
        band_w = min(tq + 2 * WINDOW, t_lat)
        assert tq & (tq - 1) == 0
        in_specs += [
            pl.BlockSpec((None, t_lat, HEAD_DIM), lambda b, g, i: (A_WN_K + g, b, 0)),
            pl.BlockSpec((None, t_lat, HEAD_DIM), lambda b, g, i: (A_WN_V + g, b, 0)),
        ]
        args += [a_lat, a_lat]
    return pl.pallas_call(
        functools.partial(_win_kernel, has_band=has_band, band_w=band_w, t_lat=t_lat),
        grid=(batch, WIN_KV_HEADS, nq),
        in_specs=in_specs,
        out_specs=pl.BlockSpec((tq, n_rep * HEAD_DIM), lambda b, g, i: (b * nq + i, g)),
        out_shape=jax.ShapeDtypeStruct((n_q, GROUP_WIDTH), BF16),
        compiler_params=_cparams(("parallel", "parallel", "arbitrary")),
    )(*args)


def _outproj_kernel(a0_ref, a1_ref, a2_ref, a3_ref, w_ref, x_ref, mod_ref, gpost_ref, gpre_ref,
                    xo_ref, h_ref):
    piece = x_ref.shape[0] // OUTPROJ_PIECES

    def project(p):
        rows = slice(piece * p, piece * (p + 1))
        mixed = jnp.concatenate([a_ref[rows, :] for a_ref in (a0_ref, a1_ref, a2_ref, a3_ref)], axis=-1)
        return _dot(mixed, w_ref[...])

    pending = project(0)
    for p in range(OUTPROJ_PIECES):
        rows = slice(piece * p, piece * (p + 1))
        y = pending
        if p + 1 < OUTPROJ_PIECES:
            pending = project(p + 1)
        xm = x_ref[rows, :] + mod_ref[2:3, :] * _rms(y, gpost_ref[...])
        xo_ref[rows, :] = xm
        h_ref[rows, :] = (
            _rms(xm, gpre_ref[...]) * (1.0 + mod_ref[4:5, :]) + mod_ref[3:4, :]).astype(BF16)


def _outproj(mix, w_out, x, mod, g_post, g_pre, tm, mod_idx):
    n, d = x.shape
    const = lambda i: (0, 0)
    return pl.pallas_call(
        _outproj_kernel,
        grid=(n // tm,),
        in_specs=[pl.BlockSpec((tm, GROUP_WIDTH), lambda i: (i, 0))] * 4 + [
            pl.BlockSpec(w_out.shape, const),
            pl.BlockSpec((tm, d), lambda i: (i, 0)),
            pl.BlockSpec((None, 6, d), lambda i: (mod_idx(i), 0, 0)),
            pl.BlockSpec((1, d), const),
            pl.BlockSpec((1, d), const),
        ],
        out_specs=[pl.BlockSpec((tm, d), lambda i: (i, 0)), pl.BlockSpec((tm, d), lambda i: (i, 0))],
        out_shape=[jax.ShapeDtypeStruct((n, d), F32), jax.ShapeDtypeStruct((n, d), BF16)],
        compiler_params=_cparams(("parallel",)),
    )(*mix, w_out, x, mod, g_post, g_pre)


def _ffn_kernel(h_ref, wg_ref, wu_ref, wd_ref, x_ref, mod_ref, g_ref, o_ref, *, n_steps):
    f = pl.program_id(1)
    piece = h_ref.shape[0] // FFN_PIECES

    def gate_up(p):
        hp = h_ref[piece * p:piece * (p + 1), :]
        return _dot(hp, wg_ref[...]), _dot(hp, wu_ref[...])

    def sweep(first, finish):
        pending = gate_up(0)
        for p in range(FFN_PIECES):
            rows = slice(piece * p, piece * (p + 1))
            gate, up = pending
            if p + 1 < FFN_PIECES:
                pending = gate_up(p + 1)
            act = (gate / (1.0 + jnp.exp(-gate)) * up).astype(BF16)
            y = _dot(act, wd_ref[...])
            if not first:
                y = o_ref[rows, :] + y
            if finish:
                y = x_ref[rows, :] + mod_ref[5:6, :] * _rms(y, g_ref[...])
            o_ref[rows, :] = y

    last = n_steps - 1
    if n_steps == 1:
        sweep(True, True)
    else:
        pl.when(f == 0)(lambda: sweep(True, False))
        pl.when((f > 0) & (f < last))(lambda: sweep(False, False))
        pl.when(f == last)(lambda: sweep(False, True))


def _ffn(h, wg, wu, wd, x, mod, g_post, tm, tf, mod_idx):
    n, d = x.shape
    d_ff = wd.shape[0]
    return pl.pallas_call(
        functools.partial(_ffn_kernel, n_steps=d_ff // tf),
        grid=(n // tm, d_ff // tf),
        in_specs=[
            pl.BlockSpec((tm, d), lambda i, f: (i, 0)),
            pl.BlockSpec((d, tf), lambda i, f: (0, f)),
            pl.BlockSpec((d, tf), lambda i, f: (0, f)),
            pl.BlockSpec((tf, d), lambda i, f: (f, 0)),
            pl.BlockSpec((tm, d), lambda i, f: (i, 0)),
            pl.BlockSpec((None, 6, d), lambda i, f: (mod_idx(i), 0, 0)),
            pl.BlockSpec((1, d), lambda i, f: (0, 0)),
        ],
        out_specs=pl.BlockSpec((tm, d), lambda i, f: (i, 0)),
        out_shape=jax.ShapeDtypeStruct((n, d), F32),
        compiler_params=_cparams(("parallel", "arbitrary")),
    )(h, wg, wu, wd, x, mod, g_post)


def _rope_tables(n_tokens, dim):
    rows = n_tokens // GRID_W
    row = jnp.broadcast_to(jnp.arange(rows)[:, None], (rows, GRID_W)).reshape(-1).astype(F32)
    col = jnp.broadcast_to(jnp.arange(GRID_W)[None, :], (rows, GRID_W)).reshape(-1).astype(F32)
    quarter = dim // 4
    inv_freq = ROPE_BASE ** (-jnp.arange(quarter, dtype=F32) / quarter)
    ang = jnp.concatenate([row[:, None] * inv_freq, col[:, None] * inv_freq], axis=-1)
    cos, sin = jnp.cos(ang), jnp.sin(ang)
    reps = LANES // dim
    return (jnp.tile(jnp.concatenate([cos, cos], axis=-1), (1, reps)),
            jnp.tile(jnp.concatenate([-sin, sin], axis=-1), (1, reps)))


def _layer_weights(l, w_in, mla_w_uq, mla_w_ukv, w_out, ffn_w_gate, ffn_w_up, ffn_w_down):
    w_uq = mla_w_uq[l].reshape(MLA_Q_RANK, MLA_HEADS, MLA_QK_DIM)
    w_uq = jnp.pad(w_uq, ((0, 0), (0, 0), (0, MLA_QK_PAD - MLA_QK_DIM)))
    w_ukv = mla_w_ukv[l].reshape(MLA_KV_RANK, MLA_HEADS, 2, MLA_NOPE).transpose(0, 2, 1, 3)
    return {
        'w_in': jnp.pad(w_in[l], ((0, 0), (0, P_COLS - IN_COLS))).astype(BF16),
        'w_uq': w_uq.reshape(MLA_Q_RANK, MLA_HEADS * MLA_QK_PAD).astype(BF16),
        'w_ukv': w_ukv.reshape(MLA_KV_RANK, MLA_HEADS * (MLA_NOPE + MLA_V)).astype(BF16),
        'w_out': w_out[l].astype(BF16),
        'wg': ffn_w_gate[l].astype(BF16),
        'wu': ffn_w_up[l].astype(BF16),
        'wd': ffn_w_down[l].astype(BF16),
    }


def _mixers(lp, prep_q, prep_lat, prep_ctx, batch, q_is_ctx, t_q, lam_init):
    a_q, m_q, _ = prep_q
    a_lat, m_lat, vt_lat = prep_lat if prep_lat is not None else (None, None, None)
    a_ctx, m_ctx, vt_ctx = prep_ctx
    tq = lambda n_rep: min(FLASH_ROWS // n_rep, t_q)
    gq = _flash(a_q, A_GQ_Q, GQA_HEADS // GQA_KV_HEADS, a_lat, a_ctx, A_GQ_K, vt_lat, vt_ctx, VT_GQ,
                GQA_KV_HEADS, 2 * HEAD_DIM, batch, q_is_ctx, tq(2))
    df = _flash(a_q, A_DF_Q, 2, a_lat, a_ctx, A_DF_K, vt_lat, vt_ctx, VT_DF,
                DIFF_HEADS, HEAD_DIM, batch, q_is_ctx, tq(2), mode='diff',
                diff_params=lp['diff'], lam_init=lam_init)
    wn = _window(lp['win_sinks'], a_q, a_lat, a_ctx, batch, q_is_ctx, min(512, t_q))
    ml = _flash(m_q, M_Q, 1, m_lat, m_ctx, M_K, vt_lat, vt_ctx, VT_ML,
                MLA_HEADS, MLA_V, batch, q_is_ctx, tq(1))
    return gq, df, wn, ml


def kernel(x, c, ctx, c_ctx, ada_w, ada_b, norm_pre_mix, norm_post_mix, norm_pre_ffn, norm_post_ffn,
           w_in, gqa_q_norm, gqa_k_norm, diff_lambda_q1, diff_lambda_k1, diff_lambda_q2, diff_lambda_k2,
           diff_subln, win_sinks, mla_q_norm, mla_w_uq, mla_kv_norm, mla_w_ukv, w_out,
           ffn_w_gate, ffn_w_up, ffn_w_down):
    batch, t_lat, d = x.shape
    t_ctx = ctx.shape[1]
    n_layers = ada_w.shape[0]
    d_ff = ffn_w_gate.shape[-1]
    assert w_in.shape[-1] == IN_COLS and t_lat % GRID_W == 0

    tm_lat = min(1024, t_lat)
    tm_ctx = min(1024, batch * t_ctx)
    to_lat = min(512, t_lat)
    to_ctx = min(512, batch * t_ctx)
    tff_lat = min(512, t_lat)
    tff_ctx = min(512, batch * t_ctx)
    tf = 512 if d_ff % 512 == 0 else 256

    mod_rows = 16
    cc = jnp.zeros((mod_rows, d), F32).at[:batch].set(c).at[batch].set(c_ctx)
    mod_all = _modulation(cc, ada_w, ada_b).reshape(n_layers, mod_rows, 6, d)

    tables = _rope_tables(t_lat, HEAD_DIM) + _rope_tables(t_lat, DIFF_QK_DIM)

    x_lat = x.reshape(batch * t_lat, d)
    x_ctx = ctx.reshape(batch * t_ctx, d)
    ctx_mod = lambda i: batch

    row2 = lambda a: a.reshape(1, -1)
    for l in range(n_layers):
        last = l == n_layers - 1
        lam_init = 0.8 - 0.6 * math.exp(-0.3 * l)
        lp = _layer_weights(l, w_in, mla_w_uq, mla_w_ukv, w_out, ffn_w_gate, ffn_w_up, ffn_w_down)
        lp.update({
            'gqa_q_norm': row2(gqa_q_norm[l]), 'gqa_k_norm': row2(gqa_k_norm[l]),
            'mla_q_norm': row2(mla_q_norm[l]), 'mla_kv_norm': row2(mla_kv_norm[l]),
            'diff': (row2(diff_lambda_q1[l]), row2(diff_lambda_k1[l]), row2(diff_lambda_q2[l]),
                     row2(diff_lambda_k2[l]), row2(diff_subln[l])),
            'win_sinks': win_sinks[l],
        })
        mod = mod_all[l]
        g_pre_mix, g_post_mix = row2(norm_pre_mix[l]), row2(norm_post_mix[l])
        g_pre_ffn, g_post_ffn = row2(norm_pre_ffn[l]), row2(norm_post_ffn[l])

        def lat_mod(tile):
            return lambda i: (i * tile) // t_lat

        p_lat = _inproj(x_lat, mod, g_pre_mix, lp['w_in'], tm_lat, lat_mod(tm_lat))
        p_ctx = _inproj(x_ctx, mod, g_pre_mix, lp['w_in'], tm_ctx, ctx_mod)
        tp_lat, tp_ctx = min(KEY_CHUNK, t_lat), min(KEY_CHUNK, t_ctx)
        prep_lat = _prep(p_lat, tables, lp, tp_lat, t_lat // tp_lat)
        prep_ctx = _prep(p_ctx, None, lp, tp_ctx, 1)

        mix = _mixers(lp, prep_lat, prep_lat, prep_ctx, batch, False, t_lat, lam_init)
        x_mid, h2 = _outproj(mix, lp['w_out'], x_lat, mod, g_post_mix, g_pre_ffn, to_lat, lat_mod(to_lat))
        x_lat = _ffn(h2, lp['wg'], lp['wu'], lp['wd'], x_mid, mod, g_post_ffn, tff_lat, tf, lat_mod(tff_lat))

        if not last:
            mix_c = _mixers(lp, prep_ctx, None, prep_ctx, batch, True, t_ctx, lam_init)
            xc_mid, hc2 = _outproj(mix_c, lp['w_out'], x_ctx, mod, g_post_mix, g_pre_ffn, to_ctx, ctx_mod)
            x_ctx = _ffn(hc2, lp['wg'], lp['wu'], lp['wd'], xc_mid, mod, g_post_ffn, tff_ctx, tf, ctx_mod)

    return x_lat.reshape(batch, t_lat, d)
```

```python
import functools
import math

import jax
import jax.numpy as jnp
from jax import lax
from jax.experimental import pallas as pl
from jax.experimental.pallas import tpu as pltpu

F32 = jnp.float32
BF16 = jnp.bfloat16

GRID_W = 64
HEAD_DIM = 128
GQA_HEADS, GQA_KV_HEADS = 4, 2
DIFF_HEADS, DIFF_QK_DIM = 4, 64
WIN_HEADS, WIN_KV_HEADS, WINDOW = 4, 2, 128
MLA_HEADS, MLA_Q_RANK, MLA_KV_RANK = 4, 512, 256
MLA_NOPE, MLA_ROPE, MLA_V = 128, 64, 128
MLA_QK_DIM = MLA_NOPE + MLA_ROPE
MLA_QK_PAD = 256
GROUP_WIDTH = 512
ROPE_BASE = 10000.0
NORM_EPS = 1e-6
NEG_INF = -1e30
LOG2E = math.log2(math.e)
LANES = 128

IN_COLS = 4416
P_COLS = 4608
C_GQ_Q, C_GQ_K, C_GQ_V = 0, 512, 768
C_DF_Q, C_DF_K, C_DF_V = 1024, 1536, 2048
C_WN_Q, C_WN_K, C_WN_V = 2560, 3072, 3328
C_ML_CQ, C_ML_CKV, C_ML_KR = 3584, 4096, 4352

A_GQ_Q, A_GQ_K = 0, 4
A_DF_Q, A_DF_K = 6, 14
A_WN_Q, A_WN_K, A_WN_V = 18, 22, 24
A_SLOTS = 26
VT_GQ, VT_DF, VT_ML = 0, 2, 6
VT_SLOTS = 10
VT_ROWS = HEAD_DIM + 16
KEY_CHUNK = 512
Q_SUB = 256
FLASH_ROWS = 2048
QK_LOOKAHEAD = 8
INPROJ_PIECES = 4
INPROJ_TN = 1536
OUTPROJ_PIECES = 2
FLASH_TK = 512
FLASH_CHUNKS_PER_ITER = 8
FFN_PIECES = 4
M_Q, M_K = 0, 4
M_SLOTS = 8

QS_HD = HEAD_DIM ** -0.5 * LOG2E
QS_DIFF = DIFF_QK_DIM ** -0.5 * LOG2E
QS_MLA = MLA_QK_DIM ** -0.5 * LOG2E

VMEM_LIMIT = 56 * 1024 * 1024


def _cparams(sem):
    return pltpu.CompilerParams(dimension_semantics=sem, vmem_limit_bytes=VMEM_LIMIT)


def _rms(x, g):
    return x * lax.rsqrt(jnp.mean(x * x, axis=-1, keepdims=True) + NORM_EPS) * g


def _dot(a, b):
    return jnp.dot(a, b, preferred_element_type=F32)


def _dot_nt(a, b):
    return lax.dot_general(a, b, (((1,), (1,)), ((), ())), preferred_element_type=F32)


def _mod_kernel(c_ref, w_ref, b_ref, o_ref):
    c = c_ref[...]
    s = (c / (1.0 + jnp.exp(-c))).astype(BF16)
    o_ref[...] = _dot(s, w_ref[...].astype(BF16)) + b_ref[...]


def _modulation(cc, ada_w, ada_b):
    n_layers, d, d6 = ada_w.shape
    rows = cc.shape[0]
    tn = d // 2
    return pl.pallas_call(
        _mod_kernel,
        grid=(n_layers, d6 // tn),
        in_specs=[
            pl.BlockSpec((rows, d), lambda l, j: (0, 0)),
            pl.BlockSpec((None, d, tn), lambda l, j: (l, 0, j)),
            pl.BlockSpec((None, 1, tn), lambda l, j: (l, 0, j)),
        ],
        out_specs=pl.BlockSpec((None, rows, tn), lambda l, j: (l, 0, j)),
        out_shape=jax.ShapeDtypeStruct((n_layers, rows, d6), F32),
        compiler_params=_cparams(("parallel", "parallel")),
    )(cc, ada_w, ada_b.reshape(n_layers, 1, d6))


def _inproj_kernel(x_ref, mod_ref, g_ref, w_ref, o_ref, h_scr):
    j = pl.program_id(1)
    piece = x_ref.shape[0] // INPROJ_PIECES

    @pl.when(j == 0)
    def _():
        for p in range(INPROJ_PIECES):
            rows = slice(piece * p, piece * (p + 1))
            h = _rms(x_ref[rows, :], g_ref[...])
            h = (h * (1.0 + mod_ref[1:2, :]) + mod_ref[0:1, :]).astype(BF16)
            h_scr[rows, :] = h
            o_ref[rows, :] = _dot(h, w_ref[...]).astype(BF16)

    @pl.when(j > 0)
    def _():
        o_ref[...] = _dot(h_scr[...], w_ref[...]).astype(BF16)


def _inproj(x, mod, gain, w, tm, mod_idx):
    n, d = x.shape
    tn = INPROJ_TN
    return pl.pallas_call(
        _inproj_kernel,
        grid=(n // tm, P_COLS // tn),
        in_specs=[
            pl.BlockSpec((tm, d), lambda i, j: (i, 0)),
            pl.BlockSpec((None, 6, d), lambda i, j: (mod_idx(i), 0, 0)),
            pl.BlockSpec((1, d), lambda i, j: (0, 0)),
            pl.BlockSpec((d, tn), lambda i, j: (0, j)),
        ],
        out_specs=pl.BlockSpec((tm, tn), lambda i, j: (i, j)),
        out_shape=jax.ShapeDtypeStruct((n, P_COLS), BF16),
        scratch_shapes=[pltpu.VMEM((tm, d), BF16)],
        compiler_params=_cparams(("parallel", "arbitrary")),
    )(x, mod, gain, w)


def _prep_kernel(*refs, rotate):
    if rotate:
        p_ref, c128_ref, s128_ref, c64_ref, s64_ref = refs[:5]
        refs = refs[5:]
    else:
        p_ref = refs[0]
        refs = refs[1:]
    gq_ref, gk_ref, qn_ref, wuq_ref, kvn_ref, wukv_ref, a_ref, m_ref, vt_ref = refs
    tp = p_ref.shape[0]

    def cols(lo, width=LANES):
        return p_ref[:, lo:lo + width].astype(F32)

    if rotate:
        lane = lax.broadcasted_iota(jnp.int32, (tp, LANES), 1)
        low32 = (lane & 63) < 32

        def rope128(x):
            return x * c128_ref[...] + pltpu.roll(x, 64, 1) * s128_ref[...]

        def rope64(x):
            rot = jnp.where(low32, pltpu.roll(x, 96, 1), pltpu.roll(x, 32, 1))
            return x * c64_ref[...] + rot * s64_ref[...]
    else:
        rope128 = rope64 = lambda x: x

    first_half = lax.broadcasted_iota(jnp.int32, (tp, LANES), 1) < 64
    ones_rows = jnp.ones((VT_ROWS - HEAD_DIM, tp), BF16)

    def value_t(v):
        return jnp.concatenate([v.T.astype(BF16), ones_rows], axis=0)

    for h in range(GQA_HEADS):
        x = rope128(_rms(cols(C_GQ_Q + LANES * h), gq_ref[...]))
        a_ref[A_GQ_Q + h] = (x * QS_HD).astype(BF16)
    for g in range(GQA_KV_HEADS):
        x = rope128(_rms(cols(C_GQ_K + LANES * g), gk_ref[...]))
        a_ref[A_GQ_K + g] = x.astype(BF16)
        vt_ref[VT_GQ + g] = value_t(cols(C_GQ_V + LANES * g))

    for h in range(DIFF_HEADS):
        x = rope64(cols(C_DF_Q + LANES * h)) * QS_DIFF
        a_ref[A_DF_Q + 2 * h] = jnp.where(first_half, x, 0.0).astype(BF16)
        a_ref[A_DF_Q + 2 * h + 1] = jnp.where(first_half, 0.0, x).astype(BF16)
        a_ref[A_DF_K + h] = rope64(cols(C_DF_K + LANES * h)).astype(BF16)
        vt_ref[VT_DF + h] = value_t(cols(C_DF_V + LANES * h))

    for h in range(WIN_HEADS):
        a_ref[A_WN_Q + h] = (rope128(cols(C_WN_Q + LANES * h)) * QS_HD).astype(BF16)
    for g in range(WIN_KV_HEADS):
        a_ref[A_WN_K + g] = rope128(cols(C_WN_K + LANES * g)).astype(BF16)
        a_ref[A_WN_V + g] = p_ref[:, C_WN_V + LANES * g:C_WN_V + LANES * (g + 1)]

    hq = _rms(cols(C_ML_CQ, MLA_Q_RANK), qn_ref[...]).astype(BF16)
    qa = _dot(hq, wuq_ref[...])
    hkv = _rms(cols(C_ML_CKV, MLA_KV_RANK), kvn_ref[...]).astype(BF16)
    kva = _dot(hkv, wukv_ref[...])
    kr = rope64(cols(C_ML_KR)).astype(BF16)
    for h in range(MLA_HEADS):
        lo = MLA_QK_PAD * h
        m_ref[M_Q + h, :, 0:LANES] = (qa[:, lo:lo + LANES] * QS_MLA).astype(BF16)
        m_ref[M_Q + h, :, LANES:2 * LANES] = (
            rope64(qa[:, lo + LANES:lo + 2 * LANES]) * QS_MLA).astype(BF16)
        m_ref[M_K + h, :, 0:LANES] = kva[:, LANES * h:LANES * (h + 1)].astype(BF16)
        m_ref[M_K + h, :, LANES:2 * LANES] = kr
        v_lo = MLA_HEADS * MLA_NOPE + MLA_V * h
        vt_ref[VT_ML + h] = value_t(kva[:, v_lo:v_lo + MLA_V])


def _prep(p, tables, lp, tp, tiles_per_seq):
    n = p.shape[0]
    rotate = tables is not None
    const = lambda i: (0, 0)
    in_specs = [pl.BlockSpec((tp, P_COLS), lambda i: (i, 0))]
    args = [p]
    if rotate:
        tab = pl.BlockSpec((tp, LANES), lambda i: (i % tiles_per_seq, 0))
        in_specs += [tab] * 4
        args += list(tables)
    in_specs += [
        pl.BlockSpec((1, HEAD_DIM), const),
        pl.BlockSpec((1, HEAD_DIM), const),
        pl.BlockSpec((1, MLA_Q_RANK), const),
        pl.BlockSpec((MLA_Q_RANK, MLA_HEADS * MLA_QK_PAD), const),
        pl.BlockSpec((1, MLA_KV_RANK), const),
        pl.BlockSpec((MLA_KV_RANK, MLA_HEADS * (MLA_NOPE + MLA_V)), const),
    ]
    args += [lp['gqa_q_norm'], lp['gqa_k_norm'], lp['mla_q_norm'], lp['w_uq'],
             lp['mla_kv_norm'], lp['w_ukv']]
    return pl.pallas_call(
        functools.partial(_prep_kernel, rotate=rotate),
        grid=(n // tp,),
        in_specs=in_specs,
        out_specs=[
            pl.BlockSpec((A_SLOTS, tp, LANES), lambda i: (0, i, 0)),
            pl.BlockSpec((M_SLOTS, tp, MLA_QK_PAD), lambda i: (0, i, 0)),
            pl.BlockSpec((VT_SLOTS, None, VT_ROWS, tp), lambda i: (0, i, 0, 0)),
        ],
        out_shape=[
            jax.ShapeDtypeStruct((A_SLOTS, n, LANES), BF16),
            jax.ShapeDtypeStruct((M_SLOTS, n, MLA_QK_PAD), BF16),
            jax.ShapeDtypeStruct((VT_SLOTS, n // tp, VT_ROWS, tp), BF16),
        ],
        compiler_params=_cparams(("parallel",)),
    )(*args)


def _flash_kernel(*refs, mode, has_latent, tk, cpi, lam_init):
    q_ref, kc_ref, vct_ref = refs[:3]
    refs = refs[3:]
    if has_latent:
        kl_ref, vlt_ref = refs[:2]
        refs = refs[2:]
    if mode == 'diff':
        lq1_ref, lk1_ref, lq2_ref, lk2_ref, sub_ref = refs[:5]
        refs = refs[5:]
    o_ref, m_scr, acc_scr = refs
    n_rep, tq, dk = q_ref.shape
    dv = HEAD_DIM
    sub_per_rep = tq // Q_SUB
    n_sub = n_rep * sub_per_rep

    def q_sub(j):
        r, s = divmod(j, sub_per_rep)
        return q_ref[r, Q_SUB * s:Q_SUB * (s + 1), :]

    def pv(vt_ref, first_chunk, n_chunks, p):
        out = None
        width = vt_ref.shape[-1]
        for c in range(n_chunks):
            part = _dot(vt_ref[first_chunk + c], p[width * c:width * (c + 1)])
            out = part if out is None else out + part
        return out

    def pipeline(units):
        def scores(unit):
            return _dot_nt(unit[0](), q_sub(unit[-1]))

        pending = [scores(u) for u in units[:QK_LOOKAHEAD]]
        for idx, (_, vt_ref, first_chunk, n_chunks, first, j) in enumerate(units):
            st = pending.pop(0)
            if idx + QK_LOOKAHEAD < len(units):
                pending.append(scores(units[idx + QK_LOOKAHEAD]))
            m_new = jnp.max(st, axis=0, keepdims=True)
            if not first:
                m_old = m_scr[j]
                m_new = jnp.maximum(m_old, m_new)
                alpha = jnp.exp2(m_old - m_new)
            p = jnp.exp2(st - m_new)
            acc_new = pv(vt_ref, first_chunk, n_chunks, p.astype(BF16))
            m_scr[j] = m_new
            acc_scr[j] = acc_new if first else alpha * acc_scr[j] + acc_new

    ctx_units = [(lambda: kc_ref[...], vct_ref, 0, kc_ref.shape[0] // vct_ref.shape[-1], True, j)
                 for j in range(n_sub)]
    if not has_latent:
        pipeline(ctx_units)
    else:
        vt_per_chunk = tk // vlt_ref.shape[-1]
        n_iter = kl_ref.shape[0] // (tk * cpi)

        def latent_units(it):
            units = []
            for o in range(cpi):
                chunk = it * cpi + o
                if isinstance(chunk, int):
                    load_k = lambda lo=chunk * tk: kl_ref[lo:lo + tk, :]
                else:
                    load_k = lambda lo=chunk * tk: kl_ref[pl.ds(pl.multiple_of(lo, tk), tk), :]
                units += [(load_k, vlt_ref, chunk * vt_per_chunk, vt_per_chunk, False, j)
                          for j in range(n_sub)]
            return units

        if n_iter == 1:
            pipeline(ctx_units + latent_units(0))
        else:
            pipeline(ctx_units)

            def body(it, carry):
                pipeline(latent_units(it))
                return carry

            lax.fori_loop(0, n_iter, body, 0)

    def out_t(j):
        return acc_scr[j, 0:dv, :] / acc_scr[j, dv:dv + 1, :]

    if mode == 'plain':
        for j in range(n_sub):
            r, s = divmod(j, sub_per_rep)
            o_ref[Q_SUB * s:Q_SUB * (s + 1), dv * r:dv * (r + 1)] = out_t(j).T.astype(o_ref.dtype)
    else:
        lam = (jnp.exp(jnp.sum(lq1_ref[...] * lk1_ref[...], keepdims=True))
               - jnp.exp(jnp.sum(lq2_ref[...] * lk2_ref[...], keepdims=True)) + lam_init)
        for s in range(sub_per_rep):
            d = (out_t(s) - lam * out_t(sub_per_rep + s)).T
            o_ref[Q_SUB * s:Q_SUB * (s + 1), :] = (
                _rms(d, sub_ref[...]) * (1.0 - lam_init)).astype(o_ref.dtype)


def _flash(q_arr, q_slot, n_rep, k_lat, k_ctx, k_slot, vt_lat, vt_ctx, v_slot, n_kv, out_w,
           batch, q_is_ctx, tq, mode='plain', diff_params=None, lam_init=0.0):
    dk = q_arr.shape[-1]
    dv = vt_ctx.shape[-2]
    n_q = q_arr.shape[1]
    t_ctx = k_ctx.shape[1] // batch
    nq = n_q // batch // tq
    has_latent = not q_is_ctx
    q_blk = q_slot // n_rep
    w_ctx = vt_ctx.shape[-1]
    assert tq % Q_SUB == 0 and t_ctx % w_ctx == 0
    in_specs = [
        pl.BlockSpec((n_rep, tq, dk), lambda b, g, i: (q_blk + g, b * nq + i, 0)),
        pl.BlockSpec((None, t_ctx, dk), lambda b, g, i: (k_slot + g, b, 0)),
        pl.BlockSpec((None, t_ctx // w_ctx, dv, w_ctx), lambda b, g, i: (v_slot + g, b, 0, 0)),
    ]
    args = [q_arr, k_ctx, vt_ctx]
    tk = cpi = 0
    if has_latent:
        t_lat = k_lat.shape[1] // batch
        w_lat = vt_lat.shape[-1]
        tk = min(FLASH_TK, t_lat)
        cpi = min(FLASH_CHUNKS_PER_ITER, t_lat // tk)
        assert t_lat % (tk * cpi) == 0 and tk % w_lat == 0
        in_specs += [
            pl.BlockSpec((None, t_lat, dk), lambda b, g, i: (k_slot + g, b, 0)),
            pl.BlockSpec((None, t_lat // w_lat, dv, w_lat), lambda b, g, i: (v_slot + g, b, 0, 0)),
        ]
        args += [k_lat, vt_lat]
    if mode == 'diff':
        in_specs += [pl.BlockSpec((1, DIFF_QK_DIM), lambda b, g, i: (0, 0))] * 4
        in_specs += [pl.BlockSpec((1, HEAD_DIM), lambda b, g, i: (0, 0))]
        args += list(diff_params)
    n_sub = n_rep * tq // Q_SUB
    return pl.pallas_call(
        functools.partial(_flash_kernel, mode=mode, has_latent=has_latent, tk=tk, cpi=cpi,
                          lam_init=lam_init),
        grid=(batch, n_kv, nq),
        in_specs=in_specs,
        out_specs=pl.BlockSpec((tq, out_w), lambda b, g, i: (b * nq + i, g)),
        out_shape=jax.ShapeDtypeStruct((n_q, n_kv * out_w), BF16),
        scratch_shapes=[pltpu.VMEM((n_sub, 1, Q_SUB), F32), pltpu.VMEM((n_sub, VT_ROWS, Q_SUB), F32)],
        compiler_params=_cparams(("parallel", "parallel", "arbitrary")),
    )(*args)


def _win_kernel(*refs, has_band, band_w, t_lat):
    sink_ref, q_ref, kc_ref, vc_ref = refs[:4]
    refs = refs[4:]
    if has_band:
        kl_ref, vl_ref = refs[:2]
        refs = refs[2:]
    (o_ref,) = refs
    n_rep, tq, dk = q_ref.shape
    dv = vc_ref.shape[-1]
    sub_per_rep = tq // Q_SUB
    n_sub = n_rep * sub_per_rep
    g = pl.program_id(1)
    i = pl.program_id(2)

    kc = kc_ref[...]
    vct = vc_ref[...].astype(F32).T.astype(BF16)
    if has_band:
        start = pl.multiple_of(jnp.clip(i * tq - WINDOW, 0, t_lat - band_w), LANES)
        kb = kl_ref[pl.ds(start, band_w), :]
        vbt = vl_ref[pl.ds(start, band_w), :].astype(F32).T.astype(BF16)
        k_pos = start + lax.broadcasted_iota(jnp.int32, (band_w, Q_SUB), 0)
        q_pos = i * tq + lax.broadcasted_iota(jnp.int32, (band_w, Q_SUB), 1)

    def scores(j):
        r, s = divmod(j, sub_per_rep)
        q = q_ref[r, Q_SUB * s:Q_SUB * (s + 1), :]
        return _dot_nt(kc, q), (_dot_nt(kb, q) if has_band else None)

    pending = [scores(j) for j in range(n_sub)]
    for j in range(n_sub):
        r, s = divmod(j, sub_per_rep)
        st_c, st_b = pending[j]
        sink = sink_ref[g * n_rep + r] * LOG2E
        m = jnp.maximum(jnp.max(st_c, axis=0, keepdims=True), sink)
        if has_band:
            in_window = jnp.abs(q_pos + Q_SUB * s - k_pos) <= WINDOW
            st_b = jnp.where(in_window, st_b, NEG_INF)
            m = jnp.maximum(m, jnp.max(st_b, axis=0, keepdims=True))
        p_c = jnp.exp2(st_c - m)
        l = jnp.sum(p_c, axis=0, keepdims=True) + jnp.exp2(sink - m)
        acc = _dot(vct, p_c.astype(BF16))
        if has_band:
            p_b = jnp.exp2(st_b - m)
            l = l + jnp.sum(p_b, axis=0, keepdims=True)
            acc = acc + _dot(vbt, p_b.astype(BF16))
        o_ref[Q_SUB * s:Q_SUB * (s + 1), dv * r:dv * (r + 1)] = (acc / l).T.astype(o_ref.dtype)


def _window(sinks, a_q, a_lat, a_ctx, batch, q_is_ctx, tq):
    n_rep = WIN_HEADS // WIN_KV_HEADS
    n_q = a_q.shape[1]
    t_ctx = a_ctx.shape[1] // batch
    nq = n_q // batch // tq
    has_band = not q_is_ctx
    q_blk = A_WN_Q // n_rep
    in_specs = [
        pl.BlockSpec(memory_space=pltpu.SMEM),
        pl.BlockSpec((n_rep, tq, HEAD_DIM), lambda b, g, i: (q_blk + g, b * nq + i, 0)),
        pl.BlockSpec((None, t_ctx, HEAD_DIM), lambda b, g, i: (A_WN_K + g, b, 0)),
        pl.BlockSpec((None, t_ctx, HEAD_DIM), lambda b, g, i: (A_WN_V + g, b, 0)),
    ]
    args = [sinks, a_q, a_ctx, a_ctx]
    band_w = t_lat = 0
    if has_band:
        t_lat = a_lat.shape[1] // batch
        band_w = min(tq + 2 * WINDOW, t_lat)
        assert tq & (tq - 1) == 0
        in_specs += [
            pl.BlockSpec((None, t_lat, HEAD_DIM), lambda b, g, i: (A_WN_K + g, b, 0)),
            pl.BlockSpec((None, t_lat, HEAD_DIM), lambda b, g, i: (A_WN_V + g, b, 0)),
        ]
        args += [a_lat, a_lat]
    return pl.pallas_call(
        functools.partial(_win_kernel, has_band=has_band, band_w=band_w, t_lat=t_lat),
        grid=(batch, WIN_KV_HEADS, nq),
        in_specs=in_specs,
        out_specs=pl.BlockSpec((tq, n_rep * HEAD_DIM), lambda b, g, i: (b * nq + i, g)),
        out_shape=jax.ShapeDtypeStruct((n_q, GROUP_WIDTH), BF16),
        compiler_params=_cparams(("parallel", "parallel", "arbitrary")),
    )(*args)


def _outproj_kernel(a0_ref, a1_ref, a2_ref, a3_ref, w_ref, x_ref, mod_ref, gpost_ref, gpre_ref,
                    xo_ref, h_ref):
    piece = x_ref.shape[0] // OUTPROJ_PIECES

    def project(p):
        rows = slice(piece * p, piece * (p + 1))
        mixed = jnp.concatenate([a_ref[rows, :] for a_ref in (a0_ref, a1_ref, a2_ref, a3_ref)], axis=-1)
        return _dot(mixed, w_ref[...])

    pending = project(0)
    for p in range(OUTPROJ_PIECES):
        rows = slice(piece * p, piece * (p + 1))
        y = pending
        if p + 1 < OUTPROJ_PIECES:
            pending = project(p + 1)
        xm = x_ref[rows, :] + mod_ref[2:3, :] * _rms(y, gpost_ref[...])
        xo_ref[rows, :] = xm
        h_ref[rows, :] = (
            _rms(xm, gpre_ref[...]) * (1.0 + mod_ref[4:5, :]) + mod_ref[3:4, :]).astype(BF16)


def _outproj(mix, w_out, x, mod, g_post, g_pre, tm, mod_idx):
    n, d = x.shape
    const = lambda i: (0, 0)
    return pl.pallas_call(
        _outproj_kernel,
        grid=(n // tm,),
        in_specs=[pl.BlockSpec((tm, GROUP_WIDTH), lambda i: (i, 0))] * 4 + [
            pl.BlockSpec(w_out.shape, const),
            pl.BlockSpec((tm, d), lambda i: (i, 0)),
            pl.BlockSpec((None, 6, d), lambda i: (mod_idx(i), 0, 0)),
            pl.BlockSpec((1, d), const),
            pl.BlockSpec((1, d), const),
        ],
        out_specs=[pl.BlockSpec((tm, d), lambda i: (i, 0)), pl.BlockSpec((tm, d), lambda i: (i, 0))],
        out_shape=[jax.ShapeDtypeStruct((n, d), F32), jax.ShapeDtypeStruct((n, d), BF16)],
        compiler_params=_cparams(("parallel",)),
    )(*mix, w_out, x, mod, g_post, g_pre)


def _ffn_kernel(h_ref, wg_ref, wu_ref, wd_ref, x_ref, mod_ref, g_ref, o_ref, *, n_steps):
    f = pl.program_id(1)
    piece = h_ref.shape[0] // FFN_PIECES

    def gate_up(p):
        hp = h_ref[piece * p:piece * (p + 1), :]
        return _dot(hp, wg_ref[...]), _dot(hp, wu_ref[...])

    def sweep(first, finish):
        pending = gate_up(0)
        for p in range(FFN_PIECES):
            rows = slice(piece * p, piece * (p + 1))
            gate, up = pending
            if p + 1 < FFN_PIECES:
                pending = gate_up(p + 1)
            act = (gate / (1.0 + jnp.exp(-gate)) * up).astype(BF16)
            y = _dot(act, wd_ref[...])
            if not first:
                y = o_ref[rows, :] + y
            if finish:
                y = x_ref[rows, :] + mod_ref[5:6, :] * _rms(y, g_ref[...])
            o_ref[rows, :] = y

    last = n_steps - 1
    if n_steps == 1:
        sweep(True, True)
    else:
        pl.when(f == 0)(lambda: sweep(True, False))
        pl.when((f > 0) & (f < last))(lambda: sweep(False, False))
        pl.when(f == last)(lambda: sweep(False, True))


def _ffn(h, wg, wu, wd, x, mod, g_post, tm, tf, mod_idx):
    n, d = x.shape
    d_ff = wd.shape[0]
    return pl.pallas_call(
        functools.partial(_ffn_kernel, n_steps=d_ff // tf),
        grid=(n // tm, d_ff // tf),
        in_specs=[
            pl.BlockSpec((tm, d), lambda i, f: (i, 0)),
            pl.BlockSpec((d, tf), lambda i, f: (0, f)),
            pl.BlockSpec((d, tf), lambda i, f: (0, f)),
            pl.BlockSpec((tf, d), lambda i, f: (f, 0)),
            pl.BlockSpec((tm, d), lambda i, f: (i, 0)),
            pl.BlockSpec((None, 6, d), lambda i, f: (mod_idx(i), 0, 0)),
            pl.BlockSpec((1, d), lambda i, f: (0, 0)),
        ],
        out_specs=pl.BlockSpec((tm, d), lambda i, f: (i, 0)),
        out_shape=jax.ShapeDtypeStruct((n, d), F32),
        compiler_params=_cparams(("parallel", "arbitrary")),
    )(h, wg, wu, wd, x, mod, g_post)


def _rope_tables(n_tokens, dim):
    rows = n_tokens // GRID_W
    row = jnp.broadcast_to(jnp.arange(rows)[:, None], (rows, GRID_W)).reshape(-1).astype(F32)
    col = jnp.broadcast_to(jnp.arange(GRID_W)[None, :], (rows, GRID_W)).reshape(-1).astype(F32)
    quarter = dim // 4
    inv_freq = ROPE_BASE ** (-jnp.arange(quarter, dtype=F32) / quarter)
    ang = jnp.concatenate([row[:, None] * inv_freq, col[:, None] * inv_freq], axis=-1)
    cos, sin = jnp.cos(ang), jnp.sin(ang)
    reps = LANES // dim
    return (jnp.tile(jnp.concatenate([cos, cos], axis=-1), (1, reps)),
            jnp.tile(jnp.concatenate([-sin, sin], axis=-1), (1, reps)))


def _layer_weights(l, w_in, mla_w_uq, mla_w_ukv, w_out, ffn_w_gate, ffn_w_up, ffn_w_down):
    w_uq = mla_w_uq[l].reshape(MLA_Q_RANK, MLA_HEADS, MLA_QK_DIM)
    w_uq = jnp.pad(w_uq, ((0, 0), (0, 0), (0, MLA_QK_PAD - MLA_QK_DIM)))
    w_ukv = mla_w_ukv[l].reshape(MLA_KV_RANK, MLA_HEADS, 2, MLA_NOPE).transpose(0, 2, 1, 3)
    return {
        'w_in': jnp.pad(w_in[l], ((0, 0), (0, P_COLS - IN_COLS))).astype(BF16),
        'w_uq': w_uq.reshape(MLA_Q_RANK, MLA_HEADS * MLA_QK_PAD).astype(BF16),
        'w_ukv': w_ukv.reshape(MLA_KV_RANK, MLA_HEADS * (MLA_NOPE + MLA_V)).astype(BF16),
        'w_out': w_out[l].astype(BF16),
        'wg': ffn_w_gate[l].astype(BF16),
        'wu': ffn_w_up[l].astype(BF16),
        'wd': ffn_w_down[l].astype(BF16),
    }


def _mixers(lp, prep_q, prep_lat, prep_ctx, batch, q_is_ctx, t_q, lam_init):
    a_q, m_q, _ = prep_q
    a_lat, m_lat, vt_lat = prep_lat if prep_lat is not None else (None, None, None)
    a_ctx, m_ctx, vt_ctx = prep_ctx
    tq = lambda n_rep: min(FLASH_ROWS // n_rep, t_q)
    gq = _flash(a_q, A_GQ_Q, GQA_HEADS // GQA_KV_HEADS, a_lat, a_ctx, A_GQ_K, vt_lat, vt_ctx, VT_GQ,
                GQA_KV_HEADS, 2 * HEAD_DIM, batch, q_is_ctx, tq(2))
    df = _flash(a_q, A_DF_Q, 2, a_lat, a_ctx, A_DF_K, vt_lat, vt_ctx, VT_DF,
                DIFF_HEADS, HEAD_DIM, batch, q_is_ctx, tq(2), mode='diff',
                diff_params=lp['diff'], lam_init=lam_init)
    wn = _window(lp['win_sinks'], a_q, a_lat, a_ctx, batch, q_is_ctx, min(512, t_q))
    ml = _flash(m_q, M_Q, 1, m_lat, m_ctx, M_K, vt_lat, vt_ctx, VT_ML,
                MLA_HEADS, MLA_V, batch, q_is_ctx, tq(1))
    return gq, df, wn, ml


def kernel(x, c, ctx, c_ctx, ada_w, ada_b, norm_pre_mix, norm_post_mix, norm_pre_ffn, norm_post_ffn,
           w_in, gqa_q_norm, gqa_k_norm, diff_lambda_q1, diff_lambda_k1, diff_lambda_q2, diff_lambda_k2,
           diff_subln, win_sinks, mla_q_norm, mla_w_uq, mla_kv_norm, mla_w_ukv, w_out,
           ffn_w_gate, ffn_w_up, ffn_w_down):
    batch, t_lat, d = x.shape
    t_ctx = ctx.shape[1]
    n_layers = ada_w.shape[0]
    d_ff = ffn_w_gate.shape[-1]
    assert w_in.shape[-1] == IN_COLS and t_lat % GRID_W == 0

    tm_lat = min(1024, t_lat)
    tm_ctx = min(1024, batch * t_ctx)
    to_lat = min(512, t_lat)
    to_ctx = min(512, batch * t_ctx)
    tff_lat = min(1024, t_lat)
    tff_ctx = min(1024, batch * t_ctx)
    tf = 256

    mod_rows = 16
    cc = jnp.zeros((mod_rows, d), F32).at[:batch].set(c).at[batch].set(c_ctx)
    mod_all = _modulation(cc, ada_w, ada_b).reshape(n_layers, mod_rows, 6, d)

    tables = _rope_tables(t_lat, HEAD_DIM) + _rope_tables(t_lat, DIFF_QK_DIM)

    x_lat = x.reshape(batch * t_lat, d)
    x_ctx = ctx.reshape(batch * t_ctx, d)
    ctx_mod = lambda i: batch

    row2 = lambda a: a.reshape(1, -1)
    for l in range(n_layers):
        last = l == n_layers - 1
        lam_init = 0.8 - 0.6 * math.exp(-0.3 * l)
        lp = _layer_weights(l, w_in, mla_w_uq, mla_w_ukv, w_out, ffn_w_gate, ffn_w_up, ffn_w_down)
        lp.update({
            'gqa_q_norm': row2(gqa_q_norm[l]), 'gqa_k_norm': row2(gqa_k_norm[l]),
            'mla_q_norm': row2(mla_q_norm[l]), 'mla_kv_norm': row2(mla_kv_norm[l]),
            'diff': (row2(diff_lambda_q1[l]), row2(diff_lambda_k1[l]), row2(diff_lambda_q2[l]),
                     row2(diff_lambda_k2[l]), row2(diff_subln[l])),
            'win_sinks': win_sinks[l],
        })
        mod = mod_all[l]
        g_pre_mix, g_post_mix = row2(norm_pre_mix[l]), row2(norm_post_mix[l])
        g_pre_ffn, g_post_ffn = row2(norm_pre_ffn[l]), row2(norm_post_ffn[l])

        def lat_mod(tile):
            return lambda i: (i * tile) // t_lat

        p_lat = _inproj(x_lat, mod, g_pre_mix, lp['w_in'], tm_lat, lat_mod(tm_lat))
        p_ctx = _inproj(x_ctx, mod, g_pre_mix, lp['w_in'], tm_ctx, ctx_mod)
        tp_lat, tp_ctx = min(KEY_CHUNK, t_lat), min(KEY_CHUNK, t_ctx)
        prep_lat = _prep(p_lat, tables, lp, tp_lat, t_lat // tp_lat)
        prep_ctx = _prep(p_ctx, None, lp, tp_ctx, 1)

        mix = _mixers(lp, prep_lat, prep_lat, prep_ctx, batch, False, t_lat, lam_init)
        x_mid, h2 = _outproj(mix, lp['w_out'], x_lat, mod, g_post_mix, g_pre_ffn, to_lat, lat_mod(to_lat))
        x_lat = _ffn(h2, lp['wg'], lp['wu'], lp['wd'], x_mid, mod, g_post_ffn, tff_lat, tf, lat_mod(tff_lat))

        if not last:
            mix_c = _mixers(lp, prep_ctx, None, prep_ctx, batch, True, t_ctx, lam_init)
            xc_mid, hc2 = _outproj(mix_c, lp['w_out'], x_ctx, mod, g_post_mix, g_pre_ffn, to_ctx, ctx_mod)
            x_ctx = _ffn(hc2, lp['wg'], lp['wu'], lp['wd'], xc_mid, mod, g_post_ffn, tff_ctx, tf, ctx_mod)

    return x_lat.reshape(batch, t_lat, d)
```
